```python
import jax, jax.numpy as jnp
from jax import lax
import numpy as np

D_MODEL = 1024
BATCH = 8
SEQ = 2048
DEPTH = 1

GRID_W = 64
N_ATT_HEADS = 8
ATT_HEAD_DIM = 64
D_ATT = N_ATT_HEADS * ATT_HEAD_DIM
WIN_H_MAX = 8
WIN_W = 16
Q_BLOCK_W = 16
KV_BLOCK_W = Q_BLOCK_W + WIN_W
N_COL_BLOCKS = GRID_W // Q_BLOCK_W
D_REC = D_MODEL
N_REC_BLOCKS = 16
REC_BLOCK = D_REC // N_REC_BLOCKS
CONV_W = 4
LRU_C = 8.0
N_DIR = 2
D_FF = 4 * D_MODEL
EPS = 1e-6
D_IN = 3 * D_ATT + 2 * D_REC + 2 * D_MODEL
SPLITS = [int(v) for v in np.cumsum([D_ATT, D_ATT, D_ATT, D_REC, D_REC, D_MODEL])]

kernel_name = "hybrid_natten_rglru_gated_encoder"


def rms_norm(x, g):
    x32 = x.astype(jnp.float32)
    y = x32 * lax.rsqrt(jnp.mean(x32 * x32, axis=-1, keepdims=True) + EPS)
    return (y * g.astype(jnp.float32)).astype(x.dtype)


def neighbourhood_attention(q, k, v, rpb):
    b, s, h, dh = q.shape
    rows = s // GRID_W
    kh = min(WIN_H_MAX, rows)
    r = np.arange(rows)
    row_start = np.clip(r - kh // 2, 0, rows - kh)
    row_idx = row_start[:, None] + np.arange(kh)[None, :]
    n = np.arange(N_COL_BLOCKS)
    col_start = np.clip(n * Q_BLOCK_W - WIN_W // 2, 0, GRID_W - KV_BLOCK_W)
    col_idx = col_start[:, None] + np.arange(KV_BLOCK_W)[None, :]
    qc = n[:, None] * Q_BLOCK_W + np.arange(Q_BLOCK_W)[None, :]
    win_start = np.clip(qc - WIN_W // 2, 0, GRID_W - WIN_W)
    kc = col_idx[:, None, :]
    valid = (kc >= win_start[..., None]) & (kc < win_start[..., None] + WIN_W)
    d_row = row_idx - r[:, None] + (WIN_H_MAX - 1)
    d_col = np.clip(kc - qc[..., None], -(WIN_W - 1), WIN_W - 1) + (WIN_W - 1)

    scale = ATT_HEAD_DIM ** -0.5
    q_blk = (q * scale).reshape(b, rows, N_COL_BLOCKS, Q_BLOCK_W, h, dh).transpose(0, 4, 1, 2, 3, 5)
    k_grid = k.reshape(b, rows, GRID_W, h, dh).transpose(0, 3, 1, 2, 4)
    v_grid = v.reshape(b, rows, GRID_W, h, dh).transpose(0, 3, 1, 2, 4)
    ri = row_idx[:, None, :, None]
    ci = col_idx[None, :, None, :]
    kg = k_grid[:, :, ri, ci]
    vg = v_grid[:, :, ri, ci]

    scores = jnp.einsum('bhrnqd,bhrnikd->bhrnqik', q_blk, kg).astype(jnp.float32)
    bias = rpb.astype(jnp.float32)[:, d_row[:, None, None, :, None], d_col[None, :, :, None, :]]
    scores = scores + bias[None]
    scores = jnp.where(valid[:, :, None, :], scores, -1e30)
    probs = jax.nn.softmax(scores, axis=(-2, -1)).astype(v.dtype)
    out = jnp.einsum('bhrnqik,bhrnikd->bhrnqd', probs, vg)
    return out.transpose(0, 2, 3, 4, 1, 5).reshape(b, s, h * dh)


def centred_depthwise_conv(u, w, bias):
    s = u.shape[1]
    left = CONV_W // 2
    right = CONV_W - 1 - left
    up = jnp.pad(u, ((0, 0), (left, right), (0, 0)))
    out = bias
    for j in range(CONV_W):
        out = out + up[:, j:j + s] * w[j]
    return out


def block_diag_linear(u, w, b):
    bsz, s, c = u.shape
    ub = u.reshape(bsz, s, N_REC_BLOCKS, REC_BLOCK)
    return jnp.einsum('bsnc,ncd->bsnd', ub, w).reshape(bsz, s, c) + b


def rg_lru(u, w_a, b_a, w_i, b_i, lam, reverse):
    r_gate = jax.nn.sigmoid(block_diag_linear(u, w_a, b_a)).astype(jnp.float32)
    i_gate = jax.nn.sigmoid(block_diag_linear(u, w_i, b_i))
    log_a = -LRU_C * r_gate * jax.nn.softplus(-lam.astype(jnp.float32))
    a = jnp.exp(log_a)
    mult = jnp.sqrt(jnp.maximum(-jnp.expm1(2.0 * log_a), 0.0))
    bx = mult * (i_gate * u).astype(jnp.float32)

    def combine(c1, c2):
        a1, b1 = c1
        a2, b2 = c2
        return a1 * a2, a2 * b1 + b2

    _, h = lax.associative_scan(combine, (a, bx), axis=1, reverse=reverse)
    return h.astype(u.dtype)


def setup_inputs(seed: int = 0) -> dict:
    key = jax.random.key(seed)
    ks = jax.random.split(key, 20)
    f32 = jnp.float32
    nrm = lambda k, shape, fan_in: jax.random.normal(k, shape, f32) * (fan_in ** -0.5)
    x = jax.random.normal(ks[0], (BATCH, SEQ, D_MODEL), f32)
    ln1_g = 1.0 + 0.05 * jax.random.normal(ks[1], (DEPTH, D_MODEL), f32)
    w_in = nrm(ks[2], (DEPTH, D_MODEL, D_IN), D_MODEL)
    b_in = 0.02 * jax.random.normal(ks[3], (DEPTH, D_IN), f32)
    rpb = 0.02 * jax.random.normal(ks[4], (DEPTH, N_ATT_HEADS, 2 * WIN_H_MAX - 1, 2 * WIN_W - 1), f32)
    w_att_o = nrm(ks[5], (DEPTH, D_ATT, D_MODEL), D_ATT)
    conv_w = nrm(ks[6], (DEPTH, CONV_W, D_REC), CONV_W)
    conv_b = 0.02 * jax.random.normal(ks[7], (DEPTH, D_REC), f32)
    w_rg_a = nrm(ks[8], (DEPTH, N_DIR, N_REC_BLOCKS, REC_BLOCK, REC_BLOCK), REC_BLOCK)
    b_rg_a = 0.02 * jax.random.normal(ks[9], (DEPTH, N_DIR, D_REC), f32)
    w_rg_i = nrm(ks[10], (DEPTH, N_DIR, N_REC_BLOCKS, REC_BLOCK, REC_BLOCK), REC_BLOCK)
    b_rg_i = 0.02 * jax.random.normal(ks[11], (DEPTH, N_DIR, D_REC), f32)
    a_c = jax.random.uniform(ks[12], (DEPTH, N_DIR, D_REC), f32, 0.9, 0.999)
    a0 = a_c ** (1.0 / LRU_C)
    lru_lambda = jnp.log(a0) - jnp.log1p(-a0)
    w_rec_o = nrm(ks[13], (DEPTH, D_REC, D_MODEL), D_REC)
    w_out = nrm(ks[14], (DEPTH, D_MODEL, D_MODEL), D_MODEL)
    ln2_g = 1.0 + 0.05 * jax.random.normal(ks[15], (DEPTH, D_MODEL), f32)
    w_ff1 = nrm(ks[16], (DEPTH, D_MODEL, D_FF), D_MODEL)
    w_ff2 = nrm(ks[17], (DEPTH, D_FF, D_MODEL), D_FF)
    lnf_g = 1.0 + 0.05 * jax.random.normal(ks[18], (D_MODEL,), f32)
    return {"x": x, "ln1_g": ln1_g, "w_in": w_in, "b_in": b_in, "rpb": rpb,
            "w_att_o": w_att_o, "conv_w": conv_w, "conv_b": conv_b,
            "w_rg_a": w_rg_a, "b_rg_a": b_rg_a, "w_rg_i": w_rg_i, "b_rg_i": b_rg_i,
            "lru_lambda": lru_lambda, "w_rec_o": w_rec_o, "w_out": w_out,
            "ln2_g": ln2_g, "w_ff1": w_ff1, "w_ff2": w_ff2, "lnf_g": lnf_g}


def reference(x, ln1_g, w_in, b_in, rpb, w_att_o, conv_w, conv_b, w_rg_a, b_rg_a,
              w_rg_i, b_rg_i, lru_lambda, w_rec_o, w_out, ln2_g, w_ff1, w_ff2, lnf_g):
    b, s, _ = x.shape
    for l in range(DEPTH):
        h = rms_norm(x, ln1_g[l])
        z = h @ w_in[l] + b_in[l]
        q, k, v, u, y_branch, g_att, g_rec = jnp.split(z, SPLITS, axis=-1)

        q = q.reshape(b, s, N_ATT_HEADS, ATT_HEAD_DIM)
        k = k.reshape(b, s, N_ATT_HEADS, ATT_HEAD_DIM)
        v = v.reshape(b, s, N_ATT_HEADS, ATT_HEAD_DIM)
        y_att = neighbourhood_attention(q, k, v, rpb[l]) @ w_att_o[l]

        u = centred_depthwise_conv(u, conv_w[l], conv_b[l])
        h_fwd = rg_lru(u, w_rg_a[l, 0], b_rg_a[l, 0], w_rg_i[l, 0], b_rg_i[l, 0], lru_lambda[l, 0], False)
        h_bwd = rg_lru(u, w_rg_a[l, 1], b_rg_a[l, 1], w_rg_i[l, 1], b_rg_i[l, 1], lru_lambda[l, 1], True)
        y_rec = ((h_fwd + h_bwd) * jax.nn.gelu(y_branch)) @ w_rec_o[l]

        mixed = jax.nn.sigmoid(g_att) * y_att + jax.nn.sigmoid(g_rec) * y_rec
        x = x + mixed @ w_out[l]

        h2 = rms_norm(x, ln2_g[l])
        x = x + jnp.square(jax.nn.relu(h2 @ w_ff1[l])) @ w_ff2[l]
    return rms_norm(x, lnf_g)
```

```python
import functools

import jax
import jax.numpy as jnp
import numpy as np
from jax import lax
from jax.experimental import pallas as pl
from jax.experimental.pallas import tpu as pltpu

F32 = jnp.float32
BF16 = jnp.bfloat16

GRID_W = 64
N_HEADS = 8
HEAD_DIM = 64
D_ATT = N_HEADS * HEAD_DIM
WIN_H = 8
WIN_W = 16
N_REC_BLOCKS = 16
CONV_W = 4
LRU_C = 8.0
EPS = 1e-6
NEG = -1e30

LANES = 128
SUBLANES = 8
BF16_ROWS = 16
MXU_TILE = 256
VMEM_LIMIT = 56 * 1024 * 1024

ROW_BLOCK = 512
SCAN_STEPS = 64
ATT_BAND = 4
ATT_KEY_ROWS = 12
HEADS_PER_VREG = LANES // HEAD_DIM


def _rms(x, g):
    ms = jnp.mean(x * x, axis=-1, keepdims=True)
    return x * lax.rsqrt(ms + EPS) * g


def _sigmoid(x):
    return 1.0 / (1.0 + jnp.exp(-x))


def _gelu_tanh(x):
    c = np.sqrt(2.0 / np.pi).astype(np.float32)
    return 0.5 * x * (1.0 + jnp.tanh(c * (x + 0.044715 * (x * x * x))))


def _softplus(x):
    return jnp.maximum(x, 0.0) + jnp.log1p(jnp.exp(-jnp.abs(x)))


def _const_spec(shape):
    nd = len(shape)
    return pl.BlockSpec(shape, lambda *_: (0,) * nd)


def _inproj_kernel(x_ref, g_ref, w_ref, b_ref, q_ref, k_ref, v_ref, u_ref, gy_ref, sga_ref, sgr_ref,
                   *, d_rec, d_model):
    h = _rms(x_ref[...], g_ref[...]).astype(BF16)

    def proj(lo, width):
        return jnp.dot(h, w_ref[:, lo:lo + width], preferred_element_type=F32) + b_ref[:, lo:lo + width]

    q_ref[...] = (proj(0, D_ATT) * (HEAD_DIM ** -0.5)).astype(BF16)
    k_ref[...] = proj(D_ATT, D_ATT).astype(BF16)
    v_ref[...] = proj(2 * D_ATT, D_ATT).astype(BF16)
    lo = 3 * D_ATT
    u_ref[...] = proj(lo, d_rec).astype(BF16)
    lo += d_rec
    gy_ref[...] = _gelu_tanh(proj(lo, d_rec)).astype(BF16)
    lo += d_rec
    sga_ref[...] = _sigmoid(proj(lo, d_model)).astype(BF16)
    lo += d_model
    sgr_ref[...] = _sigmoid(proj(lo, d_model)).astype(BF16)


def _inproj(x, g, w, b, d_rec):
    bsz, s, d = x.shape
    d_in = w.shape[1]
    nt = s // ROW_BLOCK

    def tm(width):
        return jax.ShapeDtypeStruct((s, bsz * width), BF16), pl.BlockSpec((ROW_BLOCK, width), lambda bi, ti: (ti, bi))

    outs = [tm(D_ATT), tm(D_ATT), tm(D_ATT), tm(d_rec), tm(d_rec), tm(d), tm(d)]
    return pl.pallas_call(
        functools.partial(_inproj_kernel, d_rec=d_rec, d_model=d),
        grid=(bsz, nt),
        in_specs=[
            pl.BlockSpec((None, ROW_BLOCK, d), lambda bi, ti: (bi, ti, 0)),
            _const_spec((1, d)),
            _const_spec((d, d_in)),
            _const_spec((1, d_in)),
        ],
        out_specs=[o[1] for o in outs],
        out_shape=[o[0] for o in outs],
        compiler_params=pltpu.CompilerParams(
            dimension_semantics=("parallel", "parallel"), vmem_limit_bytes=VMEM_LIMIT),
        name="inproj",
    )(x, g, w, b)


def _attn_kernel(q_ref, k0_ref, k1_ref, k2_ref, v0_ref, v1_ref, v2_ref, tlo_ref, thi_ref, o_ref, bias_ref,
                 *, n_rows):
    band = pl.program_id(0)
    rows_q = ATT_BAND * GRID_W
    lane_row = lax.broadcasted_iota(jnp.int32, (GRID_W, LANES), 1) < HEAD_DIM

    @pl.when(pl.program_id(1) == 0)
    def _():
        r0 = band * ATT_BAND
        k0 = jnp.clip(r0 - WIN_H // 2, 0, n_rows - ATT_KEY_ROWS)
        for i in range(ATT_BAND):
            rq = r0 + i
            rs = jnp.clip(rq - WIN_H // 2, 0, n_rows - WIN_H)
            for jp in range(ATT_KEY_ROWS // HEADS_PER_VREG):
                rka = k0 + 2 * jp
                rkb = rka + 1
                va = jnp.logical_and(rka >= rs, rka < rs + WIN_H)
                vb = jnp.logical_and(rkb >= rs, rkb < rs + WIN_H)
                dra = jnp.clip(rka - rq + (WIN_H - 1), 0, 2 * WIN_H - 2)
                drb = jnp.clip(rkb - rq + (WIN_H - 1), 0, 2 * WIN_H - 2)
                for h in range(N_HEADS):
                    a = jnp.where(va, tlo_ref[h, dra], NEG)
                    b = jnp.where(vb, thi_ref[h, drb], NEG)
                    bias_ref[h, i * GRID_W:(i + 1) * GRID_W, jp * LANES:(jp + 1) * LANES] = jnp.where(lane_row, a, b)

    first = lax.broadcasted_iota(jnp.int32, (rows_q, LANES), 1) < HEAD_DIM
    k_refs = (k0_ref, k1_ref, k2_ref)
    v_refs = (v0_ref, v1_ref, v2_ref)
    zero = jnp.zeros((), BF16)
    for j in range(N_HEADS // HEADS_PER_VREG):
        cs = slice(j * LANES, (j + 1) * LANES)
        q2 = q_ref[:, cs]
        ks = [r[:, cs] for r in k_refs]
        vs = [r[:, cs] for r in v_refs]
        outs = []
        for half in range(HEADS_PER_VREG):
            qh = jnp.where(first if half == 0 else jnp.logical_not(first), q2, zero)
            s = jnp.concatenate(
                [lax.dot_general(qh, kk, (((1,), (1,)), ((), ())), preferred_element_type=F32) for kk in ks],
                axis=1)
            s = s + bias_ref[HEADS_PER_VREG * j + half]
            m = jnp.max(s, axis=-1, keepdims=True)
            p = jnp.exp(s - m)
            l = jnp.sum(p, axis=-1, keepdims=True)
            pb = p.astype(BF16)
            o = jnp.dot(pb[:, 0:MXU_TILE], vs[0], preferred_element_type=F32)
            for jj in range(1, len(vs)):
                o = o + jnp.dot(pb[:, jj * MXU_TILE:(jj + 1) * MXU_TILE], vs[jj], preferred_element_type=F32)
            outs.append(o / l)
        o_ref[:, cs] = jnp.where(first, outs[0], outs[1]).astype(BF16)


def _attention(q, k, v, tlo, thi, bsz):
    s = q.shape[0]
    n_rows = s // GRID_W
    n_bands = n_rows // ATT_BAND
    rows_q = ATT_BAND * GRID_W
    n_kblk = ATT_KEY_ROWS * GRID_W // MXU_TILE
    assert rows_q == MXU_TILE and ATT_KEY_ROWS * GRID_W == n_kblk * MXU_TILE

    def kv_spec(jj):
        def idx(band, bi):
            return (jnp.clip(band - 1, 0, n_bands - n_kblk) + jj, bi)
        return pl.BlockSpec((MXU_TILE, D_ATT), idx)

    return pl.pallas_call(
        functools.partial(_attn_kernel, n_rows=n_rows),
        grid=(n_bands, bsz),
        in_specs=[pl.BlockSpec((rows_q, D_ATT), lambda band, bi: (band, bi))]
        + [kv_spec(jj) for jj in range(n_kblk)] * 2
        + [_const_spec(tlo.shape), _const_spec(thi.shape)],
        out_specs=pl.BlockSpec((rows_q, D_ATT), lambda band, bi: (band, bi)),
        out_shape=jax.ShapeDtypeStruct((s, bsz * D_ATT), BF16),
        scratch_shapes=[pltpu.VMEM((N_HEADS, rows_q, ATT_KEY_ROWS * GRID_W), F32)],
        compiler_params=pltpu.CompilerParams(
            dimension_semantics=("arbitrary", "arbitrary"), vmem_limit_bytes=VMEM_LIMIT),
        name="nbr_attention",
    )(q, k, k, k, v, v, v, tlo, thi)


def _bias_tables(rpb):
    qc = np.arange(GRID_W)[:, None]
    kc = np.arange(GRID_W)[None, :]
    ws = np.clip(qc - WIN_W // 2, 0, GRID_W - WIN_W)
    valid = (kc >= ws) & (kc < ws + WIN_W)
    d_col = np.clip(kc - qc, -(WIN_W - 1), WIN_W - 1) + (WIN_W - 1)
    t = jnp.where(valid[None, None], rpb.astype(F32)[:, :, d_col], NEG)
    neg = jnp.full_like(t, NEG)
    return jnp.concatenate([t, neg], axis=-1), jnp.concatenate([neg, t], axis=-1)


def _rglru_gates(uprev_ref, u_ref, unext_ref, cw_ref, cb_ref, wa_ref, ba_ref, wi_ref, bi_ref, lam_ref,
                 a_s, bx_s, t_idx, n_t):
    rows = u_ref.shape[0]
    u = u_ref[...].astype(F32)
    prev = jnp.where(t_idx == 0, 0.0, uprev_ref[...].astype(F32))
    nxt = jnp.where(t_idx == n_t - 1, 0.0, unext_ref[...].astype(F32)[0:SUBLANES])
    ucat = jnp.concatenate([prev, u, nxt], axis=0)
    conv = cb_ref[...]
    for j in range(CONV_W):
        conv = conv + ucat[j * SUBLANES:j * SUBLANES + rows] * cw_ref[j:j + 1, :]
    conv_b = conv.astype(BF16)
    n_tiles = wa_ref.shape[0]

    def gate(w_ref, b_ref):
        cols = [jnp.dot(conv_b[:, c * MXU_TILE:(c + 1) * MXU_TILE], w_ref[c], preferred_element_type=F32)
                for c in range(n_tiles)]
        return _sigmoid(jnp.concatenate(cols, axis=1) + b_ref[...])

    r_gate = gate(wa_ref, ba_ref)
    i_gate = gate(wi_ref, bi_ref)
    log_a = (-LRU_C * r_gate) * _softplus(-lam_ref[...])
    a = jnp.exp(log_a)
    a_s[...] = a
    mult = jnp.sqrt(jnp.maximum(-jnp.tanh(log_a) * (a * a + 1.0), 0.0))
    bx_s[...] = mult * (i_gate * conv)


def _rglru_fwd_kernel(uprev_ref, u_ref, unext_ref, cw_ref, cb_ref, wa_ref, ba_ref, wi_ref, bi_ref, lam_ref,
                      h_ref, carry, a_s, bx_s):
    tb = pl.program_id(0)
    n_t = pl.num_programs(0)

    @pl.when(tb == 0)
    def _():
        carry[...] = jnp.zeros_like(carry)

    _rglru_gates(uprev_ref, u_ref, unext_ref, cw_ref, cb_ref, wa_ref, ba_ref, wi_ref, bi_ref, lam_ref,
                 a_s, bx_s, tb, n_t)

    def step(t, h):
        sl = pl.ds(pl.multiple_of(t * SUBLANES, SUBLANES), SUBLANES)
        h = a_s[sl, :] * h + bx_s[sl, :]
        bx_s[sl, :] = h
        return h

    carry[...] = lax.fori_loop(0, SCAN_STEPS, step, carry[...], unroll=8)
    h_ref[...] = bx_s[...].astype(BF16)


def _rglru_bwd_kernel(uprev_ref, u_ref, unext_ref, cw_ref, cb_ref, wa_ref, ba_ref, wi_ref, bi_ref, lam_ref,
                      hf_ref, gy_ref, y_ref, carry, a_s, bx_s):
    tb = pl.program_id(0)
    n_t = pl.num_programs(0)

    @pl.when(tb == 0)
    def _():
        carry[...] = jnp.zeros_like(carry)

    _rglru_gates(uprev_ref, u_ref, unext_ref, cw_ref, cb_ref, wa_ref, ba_ref, wi_ref, bi_ref, lam_ref,
                 a_s, bx_s, n_t - 1 - tb, n_t)

    def step(i, h):
        t = SCAN_STEPS - 1 - i
        sl = pl.ds(pl.multiple_of(t * SUBLANES, SUBLANES), SUBLANES)
        h = a_s[sl, :] * h + bx_s[sl, :]
        bx_s[sl, :] = h
        return h

    carry[...] = lax.fori_loop(0, SCAN_STEPS, step, carry[...], unroll=8)
    y_ref[...] = ((hf_ref[...].astype(F32) + bx_s[...]) * gy_ref[...].astype(F32)).astype(BF16)


def _rglru_call(kernel, reverse, u2, params, extra, name):
    n_rows, c = u2.shape
    rows = SCAN_STEPS * SUBLANES
    n_t = n_rows // rows
    halo_per_block = rows // BF16_ROWS
    n_halo = n_rows // BF16_ROWS

    def tpos(tb):
        return n_t - 1 - tb if reverse else tb

    in_specs = [
        pl.BlockSpec((BF16_ROWS, c), lambda tb: (jnp.maximum(tpos(tb) * halo_per_block - 1, 0), 0)),
        pl.BlockSpec((rows, c), lambda tb: (tpos(tb), 0)),
        pl.BlockSpec((BF16_ROWS, c), lambda tb: (jnp.minimum((tpos(tb) + 1) * halo_per_block, n_halo - 1), 0)),
    ] + [_const_spec(p.shape) for p in params] + [pl.BlockSpec((rows, c), lambda tb: (tpos(tb), 0)) for _ in extra]
    return pl.pallas_call(
        kernel,
        grid=(n_t,),
        in_specs=in_specs,
        out_specs=pl.BlockSpec((rows, c), lambda tb: (tpos(tb), 0)),
        out_shape=jax.ShapeDtypeStruct((n_rows, c), BF16),
        scratch_shapes=[pltpu.VMEM((SUBLANES, c), F32), pltpu.VMEM((rows, c), F32), pltpu.VMEM((rows, c), F32)],
        compiler_params=pltpu.CompilerParams(dimension_semantics=("arbitrary",), vmem_limit_bytes=VMEM_LIMIT),
        name=name,
    )(u2, u2, u2, *params, *extra)


def _gate_tiles(w):
    nb, bw, _ = w.shape
    per = MXU_TILE // bw
    eye = jnp.eye(per, dtype=w.dtype)
    t = jnp.einsum("cpij,pq->cpiqj", w.reshape(nb // per, per, bw, bw), eye)
    return t.reshape(nb // per, MXU_TILE, MXU_TILE).astype(BF16)


def _merge_kernel(x_ref, att_ref, yr_ref, sga_ref, sgr_ref, wao_ref, wro_ref, wout_ref, o_ref):
    y_att = jnp.dot(att_ref[...], wao_ref[...], preferred_element_type=F32)
    y_rec = jnp.dot(yr_ref[...], wro_ref[...], preferred_element_type=F32)
    mixed = sga_ref[...].astype(F32) * y_att + sgr_ref[...].astype(F32) * y_rec
    o_ref[...] = x_ref[...] + jnp.dot(mixed.astype(BF16), wout_ref[...], preferred_element_type=F32)


def _merge(x, att, yr, sga, sgr, wao, wro, wout):
    bsz, s, d = x.shape
    nt = s // ROW_BLOCK

    def tm(width):
        return pl.BlockSpec((ROW_BLOCK, width), lambda bi, ti: (ti, bi))

    x_spec = pl.BlockSpec((None, ROW_BLOCK, d), lambda bi, ti: (bi, ti, 0))
    return pl.pallas_call(
        _merge_kernel,
        grid=(bsz, nt),
        in_specs=[x_spec, tm(D_ATT), tm(yr.shape[1] // bsz), tm(d), tm(d),
                  _const_spec(wao.shape), _const_spec(wro.shape), _const_spec(wout.shape)],
        out_specs=x_spec,
        out_shape=jax.ShapeDtypeStruct(x.shape, F32),
        compiler_params=pltpu.CompilerParams(
            dimension_semantics=("parallel", "parallel"), vmem_limit_bytes=VMEM_LIMIT),
        name="merge_outproj",
    )(x, att, yr, sga, sgr, wao, wro, wout)


def _ffn_kernel(x_ref, g2_ref, w1_ref, w2_ref, gf_ref, o_ref, *, ff_chunk):
    x = x_ref[...]
    h = _rms(x, g2_ref[...]).astype(BF16)
    d_ff = w1_ref.shape[1]
    acc = x
    for c in range(d_ff // ff_chunk):
        cs = slice(c * ff_chunk, (c + 1) * ff_chunk)
        hid = jnp.maximum(jnp.dot(h, w1_ref[:, cs], preferred_element_type=F32), 0.0)
        acc = acc + jnp.dot((hid * hid).astype(BF16), w2_ref[cs, :], preferred_element_type=F32)
    o_ref[...] = _rms(acc, gf_ref[...])


def _ffn(x2d, g2, w1, w2, gf):
    n, d = x2d.shape
    spec = pl.BlockSpec((ROW_BLOCK, d), lambda i: (i, 0))
    return pl.pallas_call(
        functools.partial(_ffn_kernel, ff_chunk=1024),
        grid=(n // ROW_BLOCK,),
        in_specs=[spec, _const_spec(g2.shape), _const_spec(w1.shape), _const_spec(w2.shape), _const_spec(gf.shape)],
        out_specs=spec,
        out_shape=jax.ShapeDtypeStruct((n, d), F32),
        compiler_params=pltpu.CompilerParams(dimension_semantics=("parallel",), vmem_limit_bytes=VMEM_LIMIT),
        name="ffn_final_norm",
    )(x2d, g2, w1, w2, gf)


def kernel(x, ln1_g, w_in, b_in, rpb, w_att_o, conv_w, conv_b, w_rg_a, b_rg_a, w_rg_i, b_rg_i, lru_lambda,
           w_rec_o, w_out, ln2_g, w_ff1, w_ff2, lnf_g):
    bsz, s, d = x.shape
    d_rec = conv_w.shape[-1]
    assert ln1_g.shape[0] == 1, "single-layer stack only"
    assert bsz == SUBLANES and s % SCAN_STEPS == 0 and s % (ATT_BAND * GRID_W) == 0 and s % ROW_BLOCK == 0
    assert w_in.shape[-1] == 3 * D_ATT + 2 * d_rec + 2 * d and d_rec % MXU_TILE == 0
    row = lambda p: p.reshape(1, -1).astype(F32)
    l = 0

    q, k, v, u, gy, sga, sgr = _inproj(x, row(ln1_g[l]), w_in[l].astype(BF16), row(b_in[l]), d_rec)

    tlo, thi = _bias_tables(rpb[l])
    att = _attention(q, k, v, tlo, thi, bsz)

    u2 = u.reshape(s * bsz, d_rec)
    conv_params = [conv_w[l].astype(F32), row(conv_b[l])]

    def dir_params(di):
        return conv_params + [_gate_tiles(w_rg_a[l, di]), row(b_rg_a[l, di]),
                              _gate_tiles(w_rg_i[l, di]), row(b_rg_i[l, di]), row(lru_lambda[l, di])]

    h_fwd = _rglru_call(_rglru_fwd_kernel, False, u2, dir_params(0), [], "rglru_fwd")
    y_rec_in = _rglru_call(_rglru_bwd_kernel, True, u2, dir_params(1),
                           [h_fwd, gy.reshape(s * bsz, d_rec)], "rglru_bwd")

    x1 = _merge(x, att, y_rec_in.reshape(s, bsz * d_rec), sga, sgr,
                w_att_o[l].astype(BF16), w_rec_o[l].astype(BF16), w_out[l].astype(BF16))

    out = _ffn(x1.reshape(bsz * s, d), row(ln2_g[l]), w_ff1[l].astype(BF16), w_ff2[l].astype(BF16), row(lnf_g))
    return out.reshape(bsz, s, d)
```

```python
import functools

import jax
import jax.numpy as jnp
import numpy as np
from jax import lax
from jax.experimental import pallas as pl
from jax.experimental.pallas import tpu as pltpu

F32 = jnp.float32
BF16 = jnp.bfloat16

GRID_W = 64
N_HEADS = 8
HEAD_DIM = 64
D_ATT = N_HEADS * HEAD_DIM
WIN_H = 8
WIN_W = 16
N_REC_BLOCKS = 16
CONV_W = 4
LRU_C = 8.0
EPS = 1e-6
NEG = -1e30

LANES = 128
SUBLANES = 8
BF16_ROWS = 16
MXU_TILE = 256
VMEM_LIMIT = 56 * 1024 * 1024

ROW_BLOCK = 512
SCAN_STEPS = 64
ATT_BAND = 4
ATT_KEY_ROWS = 12
HEADS_PER_VREG = LANES // HEAD_DIM


def _rms(x, g):
    ms = jnp.mean(x * x, axis=-1, keepdims=True)
    return x * lax.rsqrt(ms + EPS) * g


def _sigmoid(x):
    return 1.0 / (1.0 + jnp.exp(-x))


def _gelu_tanh(x):
    c = np.sqrt(2.0 / np.pi).astype(np.float32)
    return 0.5 * x * (1.0 + jnp.tanh(c * (x + 0.044715 * (x * x * x))))


def _softplus(x):
    return jnp.maximum(x, 0.0) + jnp.log1p(jnp.exp(-jnp.abs(x)))


def _const_spec(shape):
    nd = len(shape)
    return pl.BlockSpec(shape, lambda *_: (0,) * nd)


def _inproj_kernel(x_ref, g_ref, w_ref, b_ref, q_ref, k_ref, v_ref, u_ref, gy_ref, sga_ref, sgr_ref,
                   *, d_rec, d_model):
    h = _rms(x_ref[...], g_ref[...]).astype(BF16)

    def proj(lo, width):
        return jnp.dot(h, w_ref[:, lo:lo + width], preferred_element_type=F32) + b_ref[:, lo:lo + width]

    q_ref[...] = (proj(0, D_ATT) * (HEAD_DIM ** -0.5)).astype(BF16)
    k_ref[...] = proj(D_ATT, D_ATT).astype(BF16)
    v_ref[...] = proj(2 * D_ATT, D_ATT).astype(BF16)
    lo = 3 * D_ATT
    u_ref[...] = proj(lo, d_rec).astype(BF16)
    lo += d_rec
    gy_ref[...] = _gelu_tanh(proj(lo, d_rec)).astype(BF16)
    lo += d_rec
    sga_ref[...] = _sigmoid(proj(lo, d_model)).astype(BF16)
    lo += d_model
    sgr_ref[...] = _sigmoid(proj(lo, d_model)).astype(BF16)


def _inproj(x, g, w, b, d_rec):
    bsz, s, d = x.shape
    d_in = w.shape[1]
    nt = s // ROW_BLOCK

    def tm(width):
        return jax.ShapeDtypeStruct((s, bsz * width), BF16), pl.BlockSpec((ROW_BLOCK, width), lambda bi, ti: (ti, bi))

    outs = [tm(D_ATT), tm(D_ATT), tm(D_ATT), tm(d_rec), tm(d_rec), tm(d), tm(d)]
    return pl.pallas_call(
        functools.partial(_inproj_kernel, d_rec=d_rec, d_model=d),
        grid=(bsz, nt),
        in_specs=[
            pl.BlockSpec((None, ROW_BLOCK, d), lambda bi, ti: (bi, ti, 0)),
            _const_spec((1, d)),
            _const_spec((d, d_in)),
            _const_spec((1, d_in)),
        ],
        out_specs=[o[1] for o in outs],
        out_shape=[o[0] for o in outs],
        compiler_params=pltpu.CompilerParams(
            dimension_semantics=("parallel", "parallel"), vmem_limit_bytes=VMEM_LIMIT),
        name="inproj",
    )(x, g, w, b)


def _attn_kernel(q_ref, k0_ref, k1_ref, k2_ref, v0_ref, v1_ref, v2_ref, tlo_ref, thi_ref, o_ref, bias_ref,
                 *, n_rows):
    band = pl.program_id(0)
    rows_q = ATT_BAND * GRID_W
    lane_row = lax.broadcasted_iota(jnp.int32, (GRID_W, LANES), 1) < HEAD_DIM

    @pl.when(pl.program_id(1) == 0)
    def _():
        r0 = band * ATT_BAND
        k0 = jnp.clip(r0 - WIN_H // 2, 0, n_rows - ATT_KEY_ROWS)
        for i in range(ATT_BAND):
            rq = r0 + i
            rs = jnp.clip(rq - WIN_H // 2, 0, n_rows - WIN_H)
            for jp in range(ATT_KEY_ROWS // HEADS_PER_VREG):
                rka = k0 + 2 * jp
                rkb = rka + 1
                va = jnp.logical_and(rka >= rs, rka < rs + WIN_H)
                vb = jnp.logical_and(rkb >= rs, rkb < rs + WIN_H)
                dra = jnp.clip(rka - rq + (WIN_H - 1), 0, 2 * WIN_H - 2)
                drb = jnp.clip(rkb - rq + (WIN_H - 1), 0, 2 * WIN_H - 2)
                for h in range(N_HEADS):
                    a = jnp.where(va, tlo_ref[h, dra], NEG)
                    b = jnp.where(vb, thi_ref[h, drb], NEG)
                    bias_ref[h, i * GRID_W:(i + 1) * GRID_W, jp * LANES:(jp + 1) * LANES] = jnp.where(lane_row, a, b)

    first = lax.broadcasted_iota(jnp.int32, (rows_q, LANES), 1) < HEAD_DIM
    k_refs = (k0_ref, k1_ref, k2_ref)
    v_refs = (v0_ref, v1_ref, v2_ref)
    zero = jnp.zeros((), BF16)
    for j in range(N_HEADS // HEADS_PER_VREG):
        cs = slice(j * LANES, (j + 1) * LANES)
        q2 = q_ref[:, cs]
        ks = [r[:, cs] for r in k_refs]
        vs = [r[:, cs] for r in v_refs]
        outs = []
        for half in range(HEADS_PER_VREG):
            qh = jnp.where(first if half == 0 else jnp.logical_not(first), q2, zero)
            s = jnp.concatenate(
                [lax.dot_general(qh, kk, (((1,), (1,)), ((), ())), preferred_element_type=F32) for kk in ks],
                axis=1)
            s = s + bias_ref[HEADS_PER_VREG * j + half]
            m = jnp.max(s, axis=-1, keepdims=True)
            p = jnp.exp(s - m)
            l = jnp.sum(p, axis=-1, keepdims=True)
            pb = p.astype(BF16)
            o = jnp.dot(pb[:, 0:MXU_TILE], vs[0], preferred_element_type=F32)
            for jj in range(1, len(vs)):
                o = o + jnp.dot(pb[:, jj * MXU_TILE:(jj + 1) * MXU_TILE], vs[jj], preferred_element_type=F32)
            outs.append(o / l)
        o_ref[:, cs] = jnp.where(first, outs[0], outs[1]).astype(BF16)


def _attention(q, k, v, tlo, thi, bsz):
    s = q.shape[0]
    n_rows = s // GRID_W
    n_bands = n_rows // ATT_BAND
    rows_q = ATT_BAND * GRID_W
    n_kblk = ATT_KEY_ROWS * GRID_W // MXU_TILE
    assert rows_q == MXU_TILE and ATT_KEY_ROWS * GRID_W == n_kblk * MXU_TILE

    def kv_spec(jj):
        def idx(band, bi):
            return (jnp.clip(band - 1, 0, n_bands - n_kblk) + jj, bi)
        return pl.BlockSpec((MXU_TILE, D_ATT), idx)

    return pl.pallas_call(
        functools.partial(_attn_kernel, n_rows=n_rows),
        grid=(n_bands, bsz),
        in_specs=[pl.BlockSpec((rows_q, D_ATT), lambda band, bi: (band, bi))]
        + [kv_spec(jj) for jj in range(n_kblk)] * 2
        + [_const_spec(tlo.shape), _const_spec(thi.shape)],
        out_specs=pl.BlockSpec((rows_q, D_ATT), lambda band, bi: (band, bi)),
        out_shape=jax.ShapeDtypeStruct((s, bsz * D_ATT), BF16),
        scratch_shapes=[pltpu.VMEM((N_HEADS, rows_q, ATT_KEY_ROWS * GRID_W), F32)],
        compiler_params=pltpu.CompilerParams(
            dimension_semantics=("arbitrary", "arbitrary"), vmem_limit_bytes=VMEM_LIMIT),
        name="nbr_attention",
    )(q, k, k, k, v, v, v, tlo, thi)


def _bias_tables(rpb):
    qc = np.arange(GRID_W)[:, None]
    kc = np.arange(GRID_W)[None, :]
    ws = np.clip(qc - WIN_W // 2, 0, GRID_W - WIN_W)
    valid = (kc >= ws) & (kc < ws + WIN_W)
    d_col = np.clip(kc - qc, -(WIN_W - 1), WIN_W - 1) + (WIN_W - 1)
    onehot = (np.arange(2 * WIN_W - 1)[:, None, None] == d_col[None]).astype(np.float32)
    t = jnp.einsum("hrd,dqk->hrqk", rpb.astype(F32), onehot, precision=lax.Precision.HIGHEST)
    t = jnp.where(valid[None, None], t, NEG)
    neg = jnp.full_like(t, NEG)
    return jnp.concatenate([t, neg], axis=-1), jnp.concatenate([neg, t], axis=-1)


def _stack_batches(ref, c):
    return jnp.concatenate([ref[:, b * c:(b + 1) * c] for b in range(SUBLANES)], axis=0)


def _perm_matrices():
    rows = SCAN_STEPS * SUBLANES
    p = np.zeros((rows, rows), np.float32)
    t, b = np.meshgrid(np.arange(SCAN_STEPS), np.arange(SUBLANES), indexing="ij")
    p[(t * SUBLANES + b).ravel(), (b * SCAN_STEPS + t).ravel()] = 1.0
    halo = SUBLANES * BF16_ROWS
    prev_steps = CONV_W // 2
    pp = np.zeros((prev_steps * SUBLANES, halo), np.float32)
    pn = np.zeros((SUBLANES, halo), np.float32)
    for bb in range(SUBLANES):
        for j in range(prev_steps):
            pp[j * SUBLANES + bb, bb * BF16_ROWS + BF16_ROWS - prev_steps + j] = 1.0
        pn[bb, bb * BF16_ROWS] = 1.0
    as_bf16 = lambda m: jnp.asarray(m, BF16)
    return as_bf16(p), as_bf16(p.T), as_bf16(pp), as_bf16(pn)


def _rglru_gates(uprev_ref, u_ref, unext_ref, p_ref, pp_ref, pn_ref, cw_ref, cb_ref, wa_ref, ba_ref, wi_ref,
                 bi_ref, lam_ref, a_s, bx_s, t_idx, n_t):
    rows, c = a_s.shape
    u = jnp.dot(p_ref[...], _stack_batches(u_ref, c), preferred_element_type=F32)
    prev = jnp.dot(pp_ref[...], _stack_batches(uprev_ref, c), preferred_element_type=F32)
    nxt = jnp.dot(pn_ref[...], _stack_batches(unext_ref, c), preferred_element_type=F32)
    prev = jnp.where(t_idx == 0, 0.0, prev)
    nxt = jnp.where(t_idx == n_t - 1, 0.0, nxt)
    ucat = jnp.concatenate([prev, u, nxt], axis=0)
    conv = cb_ref[...]
    for j in range(CONV_W):
        conv = conv + ucat[j * SUBLANES:j * SUBLANES + rows] * cw_ref[j:j + 1, :]
    conv_b = conv.astype(BF16)
    n_tiles = wa_ref.shape[0]

    def gate(w_ref, b_ref):
        cols = [jnp.dot(conv_b[:, k * MXU_TILE:(k + 1) * MXU_TILE], w_ref[k], preferred_element_type=F32)
                for k in range(n_tiles)]
        return _sigmoid(jnp.concatenate(cols, axis=1) + b_ref[...])

    r_gate = gate(wa_ref, ba_ref)
    i_gate = gate(wi_ref, bi_ref)
    log_a = (-LRU_C * r_gate) * _softplus(-lam_ref[...])
    a = jnp.exp(log_a)
    a_s[...] = a
    mult = jnp.sqrt(jnp.maximum(-jnp.tanh(log_a) * (a * a + 1.0), 0.0))
    bx_s[...] = mult * (i_gate * conv)


def _scan_block(a_s, bx_s, carry, reverse):
    def step(i, h):
        t = SCAN_STEPS - 1 - i if reverse else i
        sl = pl.ds(pl.multiple_of(t * SUBLANES, SUBLANES), SUBLANES)
        h = a_s[sl, :] * h + bx_s[sl, :]
        bx_s[sl, :] = h
        return h

    carry[...] = lax.fori_loop(0, SCAN_STEPS, step, carry[...], unroll=8)


def _rglru_fwd_kernel(uprev_ref, u_ref, unext_ref, p_ref, pp_ref, pn_ref, cw_ref, cb_ref, wa_ref, ba_ref, wi_ref,
                      bi_ref, lam_ref, h_ref, carry, a_s, bx_s):
    tb = pl.program_id(0)
    n_t = pl.num_programs(0)

    @pl.when(tb == 0)
    def _():
        carry[...] = jnp.zeros_like(carry)

    _rglru_gates(uprev_ref, u_ref, unext_ref, p_ref, pp_ref, pn_ref, cw_ref, cb_ref, wa_ref, ba_ref, wi_ref,
                 bi_ref, lam_ref, a_s, bx_s, tb, n_t)
    _scan_block(a_s, bx_s, carry, reverse=False)
    h_ref[...] = bx_s[...].astype(BF16)


def _rglru_bwd_kernel(uprev_ref, u_ref, unext_ref, p_ref, pp_ref, pn_ref, cw_ref, cb_ref, wa_ref, ba_ref, wi_ref,
                      bi_ref, lam_ref, pt_ref, gy_ref, hf_ref, y_ref, carry, a_s, bx_s):
    tb = pl.program_id(0)
    n_t = pl.num_programs(0)
    c = a_s.shape[1]

    @pl.when(tb == 0)
    def _():
        carry[...] = jnp.zeros_like(carry)

    _rglru_gates(uprev_ref, u_ref, unext_ref, p_ref, pp_ref, pn_ref, cw_ref, cb_ref, wa_ref, ba_ref, wi_ref,
                 bi_ref, lam_ref, a_s, bx_s, n_t - 1 - tb, n_t)
    _scan_block(a_s, bx_s, carry, reverse=True)
    gy = jnp.dot(p_ref[...], _stack_batches(gy_ref, c), preferred_element_type=F32)
    y = ((hf_ref[...].astype(F32) + bx_s[...]) * gy).astype(BF16)
    y_nat = jnp.dot(pt_ref[...], y, preferred_element_type=F32)
    for b in range(SUBLANES):
        y_ref[:, b * c:(b + 1) * c] = y_nat[b * SCAN_STEPS:(b + 1) * SCAN_STEPS].astype(BF16)


def _rglru_call(kernel, reverse, u, params, tm_extra, sm_extra, out_step_major, name):
    s, bc = u.shape
    c = bc // SUBLANES
    rows = SCAN_STEPS * SUBLANES
    n_t = s // SCAN_STEPS
    halo_per_block = SCAN_STEPS // BF16_ROWS
    n_halo = s // BF16_ROWS

    def tpos(tb):
        return n_t - 1 - tb if reverse else tb

    tm_spec = pl.BlockSpec((SCAN_STEPS, bc), lambda tb: (tpos(tb), 0))
    sm_spec = pl.BlockSpec((rows, c), lambda tb: (tpos(tb), 0))
    in_specs = [
        pl.BlockSpec((BF16_ROWS, bc), lambda tb: (jnp.maximum(tpos(tb) * halo_per_block - 1, 0), 0)),
        tm_spec,
        pl.BlockSpec((BF16_ROWS, bc), lambda tb: (jnp.minimum((tpos(tb) + 1) * halo_per_block, n_halo - 1), 0)),
    ] + [_const_spec(p.shape) for p in params] + [tm_spec for _ in tm_extra] + [sm_spec for _ in sm_extra]
    if out_step_major:
        out_spec, out_shape = sm_spec, jax.ShapeDtypeStruct((s * SUBLANES, c), BF16)
    else:
        out_spec, out_shape = tm_spec, jax.ShapeDtypeStruct((s, bc), BF16)
    return pl.pallas_call(
        kernel,
        grid=(n_t,),
        in_specs=in_specs,
        out_specs=out_spec,
        out_shape=out_shape,
        scratch_shapes=[pltpu.VMEM((SUBLANES, c), F32), pltpu.VMEM((rows, c), F32), pltpu.VMEM((rows, c), F32)],
        compiler_params=pltpu.CompilerParams(dimension_semantics=("arbitrary",), vmem_limit_bytes=VMEM_LIMIT),
        name=name,
    )(u, u, u, *params, *tm_extra, *sm_extra)


def _gate_tiles(w):
    nb, bw, _ = w.shape
    per = MXU_TILE // bw
    eye = jnp.eye(per, dtype=w.dtype)
    t = jnp.einsum("cpij,pq->cpiqj", w.reshape(nb // per, per, bw, bw), eye)
    return t.reshape(nb // per, MXU_TILE, MXU_TILE).astype(BF16)


def _merge_kernel(x_ref, att_ref, yr_ref, sga_ref, sgr_ref, wao_ref, wro_ref, wout_ref, o_ref):
    y_att = jnp.dot(att_ref[...], wao_ref[...], preferred_element_type=F32)
    y_rec = jnp.dot(yr_ref[...], wro_ref[...], preferred_element_type=F32)
    mixed = sga_ref[...].astype(F32) * y_att + sgr_ref[...].astype(F32) * y_rec
    o_ref[...] = x_ref[...] + jnp.dot(mixed.astype(BF16), wout_ref[...], preferred_element_type=F32)


def _merge(x, att, yr, sga, sgr, wao, wro, wout):
    bsz, s, d = x.shape
    nt = s // ROW_BLOCK

    def tm(width):
        return pl.BlockSpec((ROW_BLOCK, width), lambda bi, ti: (ti, bi))

    x_spec = pl.BlockSpec((None, ROW_BLOCK, d), lambda bi, ti: (bi, ti, 0))
    return pl.pallas_call(
        _merge_kernel,
        grid=(bsz, nt),
        in_specs=[x_spec, tm(D_ATT), tm(yr.shape[1] // bsz), tm(d), tm(d),
                  _const_spec(wao.shape), _const_spec(wro.shape), _const_spec(wout.shape)],
        out_specs=x_spec,
        out_shape=jax.ShapeDtypeStruct(x.shape, F32),
        compiler_params=pltpu.CompilerParams(
            dimension_semantics=("parallel", "parallel"), vmem_limit_bytes=VMEM_LIMIT),
        name="merge_outproj",
    )(x, att, yr, sga, sgr, wao, wro, wout)


def _ffn_kernel(x_ref, g2_ref, w1_ref, w2_ref, gf_ref, o_ref, *, ff_chunk):
    x = x_ref[...]
    h = _rms(x, g2_ref[...]).astype(BF16)
    d_ff = w1_ref.shape[1]
    acc = x
    for c in range(d_ff // ff_chunk):
        cs = slice(c * ff_chunk, (c + 1) * ff_chunk)
        hid = jnp.maximum(jnp.dot(h, w1_ref[:, cs], preferred_element_type=F32), 0.0)
        acc = acc + jnp.dot((hid * hid).astype(BF16), w2_ref[cs, :], preferred_element_type=F32)
    o_ref[...] = _rms(acc, gf_ref[...])


def _ffn(x2d, g2, w1, w2, gf):
    n, d = x2d.shape
    spec = pl.BlockSpec((ROW_BLOCK, d), lambda i: (i, 0))
    return pl.pallas_call(
        functools.partial(_ffn_kernel, ff_chunk=1024),
        grid=(n // ROW_BLOCK,),
        in_specs=[spec, _const_spec(g2.shape), _const_spec(w1.shape), _const_spec(w2.shape), _const_spec(gf.shape)],
        out_specs=spec,
        out_shape=jax.ShapeDtypeStruct((n, d), F32),
        compiler_params=pltpu.CompilerParams(dimension_semantics=("parallel",), vmem_limit_bytes=VMEM_LIMIT),
        name="ffn_final_norm",
    )(x2d, g2, w1, w2, gf)


def kernel(x, ln1_g, w_in, b_in, rpb, w_att_o, conv_w, conv_b, w_rg_a, b_rg_a, w_rg_i, b_rg_i, lru_lambda,
           w_rec_o, w_out, ln2_g, w_ff1, w_ff2, lnf_g):
    bsz, s, d = x.shape
    d_rec = conv_w.shape[-1]
    assert ln1_g.shape[0] == 1, "single-layer stack only"
    assert bsz == SUBLANES and s % SCAN_STEPS == 0 and s % (ATT_BAND * GRID_W) == 0 and s % ROW_BLOCK == 0
    assert w_in.shape[-1] == 3 * D_ATT + 2 * d_rec + 2 * d and d_rec % MXU_TILE == 0
    row = lambda p: p.reshape(1, -1).astype(F32)
    l = 0

    q, k, v, u, gy, sga, sgr = _inproj(x, row(ln1_g[l]), w_in[l].astype(BF16), row(b_in[l]), d_rec)

    tlo, thi = _bias_tables(rpb[l])
    att = _attention(q, k, v, tlo, thi, bsz)

    perm, perm_t, perm_prev, perm_next = _perm_matrices()
    conv_params = [perm, perm_prev, perm_next, conv_w[l].astype(F32), row(conv_b[l])]

    def dir_params(di):
        return conv_params + [_gate_tiles(w_rg_a[l, di]), row(b_rg_a[l, di]),
                              _gate_tiles(w_rg_i[l, di]), row(b_rg_i[l, di]), row(lru_lambda[l, di])]

    h_fwd = _rglru_call(_rglru_fwd_kernel, False, u, dir_params(0), [], [], True, "rglru_fwd")
    y_rec_in = _rglru_call(_rglru_bwd_kernel, True, u, dir_params(1) + [perm_t], [gy], [h_fwd], False, "rglru_bwd")

    x1 = _merge(x, att, y_rec_in, sga, sgr,
                w_att_o[l].astype(BF16), w_rec_o[l].astype(BF16), w_out[l].astype(BF16))

    out = _ffn(x1.reshape(bsz * s, d), row(ln2_g[l]), w_ff1[l].astype(BF16), w_ff2[l].astype(BF16), row(lnf_g))
    return out.reshape(bsz, s, d)
```

```python
import functools

import jax
import jax.numpy as jnp
import numpy as np
from jax import lax
from jax.experimental import pallas as pl
from jax.experimental.pallas import tpu as pltpu

F32 = jnp.float32
BF16 = jnp.bfloat16

GRID_W = 64
N_HEADS = 8
HEAD_DIM = 64
D_ATT = N_HEADS * HEAD_DIM
WIN_H = 8
WIN_W = 16
CONV_W = 4
LRU_C = 8.0
EPS = 1e-6
NEG = -1e30
LOG2E = float(np.log2(np.e))

LANES = 128
SUBLANES = 8
BF16_ROWS = 16
MXU_TILE = 256
VMEM_LIMIT = 56 * 1024 * 1024

ROW_BLOCK = 512
SCAN_STEPS = 64
ATT_BAND = 4
ATT_KEY_ROWS = 12
HEADS_PER_VREG = LANES // HEAD_DIM


def _rms(x, g):
    ms = jnp.mean(x * x, axis=-1, keepdims=True)
    return x * lax.rsqrt(ms + EPS) * g


def _sigmoid(x):
    return 1.0 / (1.0 + jnp.exp2(x * (-LOG2E)))


def _gelu_tanh(x):
    c = float(np.sqrt(2.0 / np.pi))
    return 0.5 * x * (1.0 + jnp.tanh(c * (x + 0.044715 * (x * x * x))))


def _softplus(x):
    return jnp.maximum(x, 0.0) + jnp.log1p(jnp.exp(-jnp.abs(x)))


def _const_spec(shape):
    nd = len(shape)
    return pl.BlockSpec(shape, lambda *_: (0,) * nd, pipeline_mode=pl.Buffered(1))


def _stack_batches(ref, c):
    return jnp.concatenate([ref[:, b * c:(b + 1) * c] for b in range(SUBLANES)], axis=0)


def _store_batches(ref, val, c):
    steps = ref.shape[0]
    for b in range(SUBLANES):
        ref[:, b * c:(b + 1) * c] = val[b * steps:(b + 1) * steps].astype(ref.dtype)


def _perm_matrix():
    rows = SCAN_STEPS * SUBLANES
    p = np.zeros((rows, rows), np.float32)
    t, b = np.meshgrid(np.arange(SCAN_STEPS), np.arange(SUBLANES), indexing="ij")
    p[(t * SUBLANES + b).ravel(), (b * SCAN_STEPS + t).ravel()] = 1.0
    return p


def _inproj_kernel(x_ref, g_ref, w_ref, b_ref, p_ref, q_ref, k_ref, v_ref, sga_ref, sgr_ref, u_ref, gy_ref,
                   *, d_rec, d_model):
    rows = SCAN_STEPS * SUBLANES
    h = _rms(x_ref[...].reshape(rows, d_model), g_ref[...]).astype(BF16)
    hp = jnp.dot(p_ref[...], h, preferred_element_type=F32).astype(BF16)

    def proj(lhs, lo, width):
        return jnp.dot(lhs, w_ref[:, lo:lo + width], preferred_element_type=F32) + b_ref[:, lo:lo + width]

    _store_batches(q_ref, proj(h, 0, D_ATT) * (HEAD_DIM ** -0.5), D_ATT)
    _store_batches(k_ref, proj(h, D_ATT, D_ATT), D_ATT)
    _store_batches(v_ref, proj(h, 2 * D_ATT, D_ATT), D_ATT)
    lo = 3 * D_ATT
    u_ref[...] = proj(hp, lo, d_rec).astype(BF16)
    lo += d_rec
    gy_ref[...] = _gelu_tanh(proj(hp, lo, d_rec)).astype(BF16)
    lo += d_rec
    _store_batches(sga_ref, _sigmoid(proj(h, lo, d_model)), d_model)
    lo += d_model
    _store_batches(sgr_ref, _sigmoid(proj(h, lo, d_model)), d_model)


def _inproj(x, g, w, b, perm, d_rec):
    bsz, s, d = x.shape
    rows = SCAN_STEPS * SUBLANES

    def tm(width):
        return jax.ShapeDtypeStruct((s, bsz * width), BF16), pl.BlockSpec((SCAN_STEPS, bsz * width), lambda t: (t, 0))

    def sm(width):
        return jax.ShapeDtypeStruct((s * bsz, width), BF16), pl.BlockSpec((rows, width), lambda t: (t, 0))

    outs = [tm(D_ATT), tm(D_ATT), tm(D_ATT), tm(d), tm(d), sm(d_rec), sm(d_rec)]
    return pl.pallas_call(
        functools.partial(_inproj_kernel, d_rec=d_rec, d_model=d),
        grid=(s // SCAN_STEPS,),
        in_specs=[
            pl.BlockSpec((bsz, SCAN_STEPS, d), lambda t: (0, t, 0)),
            _const_spec(g.shape), _const_spec(w.shape), _const_spec(b.shape), _const_spec(perm.shape),
        ],
        out_specs=[o[1] for o in outs],
        out_shape=[o[0] for o in outs],
        compiler_params=pltpu.CompilerParams(dimension_semantics=("parallel",), vmem_limit_bytes=VMEM_LIMIT),
        name="inproj",
    )(x, g, w, b, perm)


def _attn_kernel(q_ref, k0_ref, k1_ref, k2_ref, v0_ref, v1_ref, v2_ref, tlo_ref, thi_ref, o_ref, bias_ref,
                 *, n_rows):
    band = pl.program_id(0)
    rows_q = ATT_BAND * GRID_W
    lane_row = lax.broadcasted_iota(jnp.int32, (GRID_W, LANES), 1) < HEAD_DIM

    @pl.when(pl.program_id(1) == 0)
    def _():
        r0 = band * ATT_BAND
        k0 = jnp.clip(r0 - WIN_H // 2, 0, n_rows - ATT_KEY_ROWS)
        for i in range(ATT_BAND):
            rq = r0 + i
            rs = jnp.clip(rq - WIN_H // 2, 0, n_rows - WIN_H)
            for jp in range(ATT_KEY_ROWS // HEADS_PER_VREG):
                rka = k0 + 2 * jp
                rkb = rka + 1
                va = jnp.logical_and(rka >= rs, rka < rs + WIN_H)
                vb = jnp.logical_and(rkb >= rs, rkb < rs + WIN_H)
                dra = jnp.clip(rka - rq + (WIN_H - 1), 0, 2 * WIN_H - 2)
                drb = jnp.clip(rkb - rq + (WIN_H - 1), 0, 2 * WIN_H - 2)
                for h in range(N_HEADS):
                    a = jnp.where(va, tlo_ref[h, dra], NEG)
                    b = jnp.where(vb, thi_ref[h, drb], NEG)
                    bias_ref[h, i * GRID_W:(i + 1) * GRID_W, jp * LANES:(jp + 1) * LANES] = jnp.where(lane_row, a, b)

    first = lax.broadcasted_iota(jnp.int32, (rows_q, LANES), 1) < HEAD_DIM
    k_refs = (k0_ref, k1_ref, k2_ref)
    v_refs = (v0_ref, v1_ref, v2_ref)
    zero = jnp.zeros((), BF16)
    for j in range(N_HEADS // HEADS_PER_VREG):
        cs = slice(j * LANES, (j + 1) * LANES)
        q2 = q_ref[:, cs]
        ks = [r[:, cs] for r in k_refs]
        vs = [r[:, cs] for r in v_refs]
        outs = []
        for half in range(HEADS_PER_VREG):
            qh = jnp.where(first if half == 0 else jnp.logical_not(first), q2, zero)
            s = jnp.concatenate(
                [lax.dot_general(qh, kk, (((1,), (1,)), ((), ())), preferred_element_type=F32) for kk in ks],
                axis=1)
            s = s + bias_ref[HEADS_PER_VREG * j + half]
            m = jnp.max(s, axis=-1, keepdims=True)
            p = jnp.exp(s - m)
            l = jnp.sum(p, axis=-1, keepdims=True)
            pb = p.astype(BF16)
            o = jnp.dot(pb[:, 0:MXU_TILE], vs[0], preferred_element_type=F32)
            for jj in range(1, len(vs)):
                o = o + jnp.dot(pb[:, jj * MXU_TILE:(jj + 1) * MXU_TILE], vs[jj], preferred_element_type=F32)
            outs.append(o / l)
        o_ref[:, cs] = jnp.where(first, outs[0], outs[1]).astype(BF16)


def _attention(q, k, v, tlo, thi, bsz):
    s = q.shape[0]
    n_rows = s // GRID_W
    n_bands = n_rows // ATT_BAND
    rows_q = ATT_BAND * GRID_W
    n_kblk = ATT_KEY_ROWS * GRID_W // MXU_TILE
    assert rows_q == MXU_TILE and ATT_KEY_ROWS * GRID_W == n_kblk * MXU_TILE

    def kv_spec(jj):
        def idx(band, bi):
            return (jnp.clip(band - 1, 0, n_bands - n_kblk) + jj, bi)
        return pl.BlockSpec((MXU_TILE, D_ATT), idx)

    return pl.pallas_call(
        functools.partial(_attn_kernel, n_rows=n_rows),
        grid=(n_bands, bsz),
        in_specs=[pl.BlockSpec((rows_q, D_ATT), lambda band, bi: (band, bi))]
        + [kv_spec(jj) for jj in range(n_kblk)] * 2
        + [_const_spec(tlo.shape), _const_spec(thi.shape)],
        out_specs=pl.BlockSpec((rows_q, D_ATT), lambda band, bi: (band, bi)),
        out_shape=jax.ShapeDtypeStruct((s, bsz * D_ATT), BF16),
        scratch_shapes=[pltpu.VMEM((N_HEADS, rows_q, ATT_KEY_ROWS * GRID_W), F32)],
        compiler_params=pltpu.CompilerParams(
            dimension_semantics=("arbitrary", "arbitrary"), vmem_limit_bytes=VMEM_LIMIT),
        name="nbr_attention",
    )(q, k, k, k, v, v, v, tlo, thi)


def _bias_tables(rpb):
    qc = np.arange(GRID_W)[:, None]
    kc = np.arange(GRID_W)[None, :]
    ws = np.clip(qc - WIN_W // 2, 0, GRID_W - WIN_W)
    valid = (kc >= ws) & (kc < ws + WIN_W)
    d_col = np.clip(kc - qc, -(WIN_W - 1), WIN_W - 1) + (WIN_W - 1)
    onehot = (np.arange(2 * WIN_W - 1)[:, None, None] == d_col[None]).astype(np.float32)
    t = jnp.einsum("hrd,dqk->hrqk", rpb.astype(F32), onehot, precision=lax.Precision.HIGHEST)
    t = jnp.where(valid[None, None], t, NEG)
    neg = jnp.full_like(t, NEG)
    return jnp.concatenate([t, neg], axis=-1), jnp.concatenate([neg, t], axis=-1)


def _rglru_gates(uprev_ref, u_ref, unext_ref, cw_ref, cb_ref, wa_ref, ba_ref, wi_ref, bi_ref, lam_ref,
                 a_s, bx_s, t_idx, n_t):
    rows = u_ref.shape[0]
    prev_rows = (CONV_W // 2) * SUBLANES
    u = u_ref[...].astype(F32)
    prev = jnp.where(t_idx == 0, 0.0, uprev_ref[...].astype(F32)[BF16_ROWS - prev_rows:])
    nxt = jnp.where(t_idx == n_t - 1, 0.0, unext_ref[...].astype(F32)[0:SUBLANES])
    ucat = jnp.concatenate([prev, u, nxt], axis=0)
    conv = cb_ref[...]
    for j in range(CONV_W):
        conv = conv + ucat[j * SUBLANES:j * SUBLANES + rows] * cw_ref[j:j + 1, :]
    conv_b = conv.astype(BF16)
    n_tiles = wa_ref.shape[0]

    def gate(w_ref, b_ref):
        cols = [jnp.dot(conv_b[:, k * MXU_TILE:(k + 1) * MXU_TILE], w_ref[k], preferred_element_type=F32)
                for k in range(n_tiles)]
        return _sigmoid(jnp.concatenate(cols, axis=1) + b_ref[...])

    r_gate = gate(wa_ref, ba_ref)
    i_gate = gate(wi_ref, bi_ref)
    log_a = r_gate * (-LRU_C * _softplus(-lam_ref[...]))
    a = jnp.exp(log_a)
    a_s[...] = a
    z = jnp.maximum(jnp.tanh(log_a) * (-1.0 - a * a), 0.0)
    mult = jnp.where(z > 0.0, z * lax.rsqrt(z), 0.0)
    bx_s[...] = mult * (i_gate * conv)


def _scan_block(a_s, bx_s, carry, reverse):
    def step(i, h):
        t = SCAN_STEPS - 1 - i if reverse else i
        sl = pl.ds(pl.multiple_of(t * SUBLANES, SUBLANES), SUBLANES)
        h = a_s[sl, :] * h + bx_s[sl, :]
        bx_s[sl, :] = h
        return h

    carry[...] = lax.fori_loop(0, SCAN_STEPS, step, carry[...], unroll=8)


def _rglru_fwd_kernel(uprev_ref, u_ref, unext_ref, cw_ref, cb_ref, wa_ref, ba_ref, wi_ref, bi_ref, lam_ref,
                      h_ref, carry, a_s, bx_s):
    tb = pl.program_id(0)
    n_t = pl.num_programs(0)

    @pl.when(tb == 0)
    def _():
        carry[...] = jnp.zeros_like(carry)

    _rglru_gates(uprev_ref, u_ref, unext_ref, cw_ref, cb_ref, wa_ref, ba_ref, wi_ref, bi_ref, lam_ref,
                 a_s, bx_s, tb, n_t)
    _scan_block(a_s, bx_s, carry, reverse=False)
    h_ref[...] = bx_s[...].astype(BF16)


def _rglru_bwd_merge_kernel(uprev_ref, u_ref, unext_ref, cw_ref, cb_ref, wa_ref, ba_ref, wi_ref, bi_ref, lam_ref,
                            gy_ref, hf_ref, pt_ref, x_ref, att_ref, sga_ref, sgr_ref, wao_ref, wro_ref, wout_ref,
                            o_ref, carry, a_s, bx_s):
    tb = pl.program_id(0)
    n_t = pl.num_programs(0)
    rows, _ = a_s.shape
    d = x_ref.shape[-1]

    @pl.when(tb == 0)
    def _():
        carry[...] = jnp.zeros_like(carry)

    _rglru_gates(uprev_ref, u_ref, unext_ref, cw_ref, cb_ref, wa_ref, ba_ref, wi_ref, bi_ref, lam_ref,
                 a_s, bx_s, n_t - 1 - tb, n_t)
    _scan_block(a_s, bx_s, carry, reverse=True)
    y = ((hf_ref[...].astype(F32) + bx_s[...]) * gy_ref[...].astype(F32)).astype(BF16)
    y_nat = jnp.dot(pt_ref[...], y, preferred_element_type=F32).astype(BF16)
    y_rec = jnp.dot(y_nat, wro_ref[...], preferred_element_type=F32)
    y_att = jnp.dot(_stack_batches(att_ref, D_ATT), wao_ref[...], preferred_element_type=F32)
    mixed = _stack_batches(sga_ref, d).astype(F32) * y_att + _stack_batches(sgr_ref, d).astype(F32) * y_rec
    out = x_ref[...].reshape(rows, d) + jnp.dot(mixed.astype(BF16), wout_ref[...], preferred_element_type=F32)
    o_ref[...] = out.reshape(o_ref.shape)


def _rglru_specs(c, n_t, n_halo, reverse):
    rows = SCAN_STEPS * SUBLANES
    halo_per_block = rows // BF16_ROWS

    def tpos(tb):
        return n_t - 1 - tb if reverse else tb

    sm_spec = pl.BlockSpec((rows, c), lambda tb: (tpos(tb), 0))
    halo_specs = [
        pl.BlockSpec((BF16_ROWS, c), lambda tb: (jnp.maximum(tpos(tb) * halo_per_block - 1, 0), 0)),
        sm_spec,
        pl.BlockSpec((BF16_ROWS, c), lambda tb: (jnp.minimum((tpos(tb) + 1) * halo_per_block, n_halo - 1), 0)),
    ]
    return tpos, sm_spec, halo_specs


def _rglru_scratch(c):
    rows = SCAN_STEPS * SUBLANES
    return [pltpu.VMEM((SUBLANES, c), F32), pltpu.VMEM((rows, c), F32), pltpu.VMEM((rows, c), F32)]


def _rglru_fwd(u, params):
    n_rows, c = u.shape
    n_t = n_rows // (SCAN_STEPS * SUBLANES)
    _, sm_spec, halo_specs = _rglru_specs(c, n_t, n_rows // BF16_ROWS, reverse=False)
    return pl.pallas_call(
        _rglru_fwd_kernel,
        grid=(n_t,),
        in_specs=halo_specs + [_const_spec(p.shape) for p in params],
        out_specs=sm_spec,
        out_shape=jax.ShapeDtypeStruct((n_rows, c), BF16),
        scratch_shapes=_rglru_scratch(c),
        compiler_params=pltpu.CompilerParams(dimension_semantics=("arbitrary",), vmem_limit_bytes=VMEM_LIMIT),
        name="rglru_fwd",
    )(u, u, u, *params)


def _rglru_bwd_merge(u, params, gy, h_fwd, perm_t, x, att, sga, sgr, wao, wro, wout):
    n_rows, c = u.shape
    bsz, s, d = x.shape
    n_t = s // SCAN_STEPS
    tpos, sm_spec, halo_specs = _rglru_specs(c, n_t, n_rows // BF16_ROWS, reverse=True)

    def tm(width):
        return pl.BlockSpec((SCAN_STEPS, bsz * width), lambda tb: (tpos(tb), 0))

    x_spec = pl.BlockSpec((bsz, SCAN_STEPS, d), lambda tb: (0, tpos(tb), 0))
    consts = [perm_t]
    weights = [wao, wro, wout]
    return pl.pallas_call(
        _rglru_bwd_merge_kernel,
        grid=(n_t,),
        in_specs=halo_specs + [_const_spec(p.shape) for p in params] + [sm_spec, sm_spec]
        + [_const_spec(p.shape) for p in consts] + [x_spec, tm(D_ATT), tm(d), tm(d)]
        + [_const_spec(p.shape) for p in weights],
        out_specs=x_spec,
        out_shape=jax.ShapeDtypeStruct(x.shape, F32),
        scratch_shapes=_rglru_scratch(c),
        compiler_params=pltpu.CompilerParams(dimension_semantics=("arbitrary",), vmem_limit_bytes=VMEM_LIMIT),
        name="rglru_bwd_merge",
    )(u, u, u, *params, gy, h_fwd, *consts, x, att, sga, sgr, *weights)


def _gate_tiles(w):
    nb, bw, _ = w.shape
    per = MXU_TILE // bw
    eye = jnp.eye(per, dtype=w.dtype)
    t = jnp.einsum("cpij,pq->cpiqj", w.reshape(nb // per, per, bw, bw), eye)
    return t.reshape(nb // per, MXU_TILE, MXU_TILE).astype(BF16)


def _ffn_kernel(x_ref, g2_ref, w1_ref, w2_ref, gf_ref, o_ref, *, ff_chunk):
    x = x_ref[...]
    h = _rms(x, g2_ref[...]).astype(BF16)
    d_ff = w1_ref.shape[1]
    acc = x
    for c in range(d_ff // ff_chunk):
        cs = slice(c * ff_chunk, (c + 1) * ff_chunk)
        hid = jnp.maximum(jnp.dot(h, w1_ref[:, cs], preferred_element_type=F32), 0.0)
        acc = acc + jnp.dot((hid * hid).astype(BF16), w2_ref[cs, :], preferred_element_type=F32)
    o_ref[...] = _rms(acc, gf_ref[...])


def _ffn(x2d, g2, w1, w2, gf):
    n, d = x2d.shape
    spec = pl.BlockSpec((ROW_BLOCK, d), lambda i: (i, 0))
    return pl.pallas_call(
        functools.partial(_ffn_kernel, ff_chunk=1024),
        grid=(n // ROW_BLOCK,),
        in_specs=[spec, _const_spec(g2.shape), _const_spec(w1.shape), _const_spec(w2.shape), _const_spec(gf.shape)],
        out_specs=spec,
        out_shape=jax.ShapeDtypeStruct((n, d), F32),
        compiler_params=pltpu.CompilerParams(dimension_semantics=("parallel",), vmem_limit_bytes=VMEM_LIMIT),
        name="ffn_final_norm",
    )(x2d, g2, w1, w2, gf)


def kernel(x, ln1_g, w_in, b_in, rpb, w_att_o, conv_w, conv_b, w_rg_a, b_rg_a, w_rg_i, b_rg_i, lru_lambda,
           w_rec_o, w_out, ln2_g, w_ff1, w_ff2, lnf_g):
    bsz, s, d = x.shape
    d_rec = conv_w.shape[-1]
    assert ln1_g.shape[0] == 1, "single-layer stack only"
    assert bsz == SUBLANES and s % SCAN_STEPS == 0 and s % (ATT_BAND * GRID_W) == 0 and (bsz * s) % ROW_BLOCK == 0
    assert w_in.shape[-1] == 3 * D_ATT + 2 * d_rec + 2 * d and d_rec % MXU_TILE == 0
    row = lambda p: p.reshape(1, -1).astype(F32)
    l = 0

    perm = _perm_matrix()
    q, k, v, sga, sgr, u, gy = _inproj(x, row(ln1_g[l]), w_in[l].astype(BF16), row(b_in[l]),
                                       jnp.asarray(perm, BF16), d_rec)

    tlo, thi = _bias_tables(rpb[l])
    att = _attention(q, k, v, tlo, thi, bsz)

    def dir_params(di):
        return [conv_w[l].astype(F32), row(conv_b[l]), _gate_tiles(w_rg_a[l, di]), row(b_rg_a[l, di]),
                _gate_tiles(w_rg_i[l, di]), row(b_rg_i[l, di]), row(lru_lambda[l, di])]

    h_fwd = _rglru_fwd(u, dir_params(0))
    x1 = _rglru_bwd_merge(u, dir_params(1), gy, h_fwd, jnp.asarray(perm.T, BF16), x, att, sga, sgr,
                          w_att_o[l].astype(BF16), w_rec_o[l].astype(BF16), w_out[l].astype(BF16))

    out = _ffn(x1.reshape(bsz * s, d), row(ln2_g[l]), w_ff1[l].astype(BF16), w_ff2[l].astype(BF16), row(lnf_g))
    return out.reshape(bsz, s, d)
```

```python
import functools

import jax
import jax.numpy as jnp
import numpy as np
from jax import lax
from jax.experimental import pallas as pl
from jax.experimental.pallas import tpu as pltpu

F32 = jnp.float32
BF16 = jnp.bfloat16

GRID_W = 64
N_HEADS = 8
HEAD_DIM = 64
D_ATT = N_HEADS * HEAD_DIM
WIN_H = 8
WIN_W = 16
CONV_W = 4
LRU_C = 8.0
EPS = 1e-6
NEG = -1e30
LOG2E = float(np.log2(np.e))

LANES = 128
SUBLANES = 8
BF16_ROWS = 16
MXU_TILE = 256
VMEM_LIMIT = 56 * 1024 * 1024

ROW_BLOCK = 512
PROJ_TILE = 256
SCAN_STEPS = 64
ATT_BAND = 4
ATT_KEY_ROWS = 12
HEADS_PER_VREG = LANES // HEAD_DIM


def _rms(x, g):
    ms = jnp.mean(x * x, axis=-1, keepdims=True)
    return x * lax.rsqrt(ms + EPS) * g


def _sigmoid(x):
    return 1.0 / (1.0 + jnp.exp2(x * (-LOG2E)))


def _gelu_tanh(x):
    c = float(np.sqrt(2.0 / np.pi))
    return 0.5 * x * (1.0 + jnp.tanh(c * (x + 0.044715 * (x * x * x))))


def _softplus(x):
    return jnp.maximum(x, 0.0) + jnp.log1p(jnp.exp(-jnp.abs(x)))


def _const_spec(shape):
    nd = len(shape)
    return pl.BlockSpec(shape, lambda *_: (0,) * nd, pipeline_mode=pl.Buffered(1))


def _stack_batches(ref, c, lo=0, width=None):
    width = c if width is None else width
    return jnp.concatenate([ref[:, b * c + lo:b * c + lo + width] for b in range(SUBLANES)], axis=0)


def _store_batches(ref, val, c, lo=0):
    steps = ref.shape[0]
    width = val.shape[1]
    for b in range(SUBLANES):
        ref[:, b * c + lo:b * c + lo + width] = val[b * steps:(b + 1) * steps].astype(ref.dtype)


def _perm_matrix():
    rows = SCAN_STEPS * SUBLANES
    p = np.zeros((rows, rows), np.float32)
    t, b = np.meshgrid(np.arange(SCAN_STEPS), np.arange(SUBLANES), indexing="ij")
    p[(t * SUBLANES + b).ravel(), (b * SCAN_STEPS + t).ravel()] = 1.0
    return p


def _halo_perm_matrices():
    prev_steps = CONV_W // 2
    halo_rows = SUBLANES * SUBLANES
    pp = np.zeros((prev_steps * SUBLANES, halo_rows), np.float32)
    pn = np.zeros((SUBLANES, halo_rows), np.float32)
    for b in range(SUBLANES):
        for j in range(prev_steps):
            pp[j * SUBLANES + b, b * SUBLANES + SUBLANES - prev_steps + j] = 1.0
        pn[b, b * SUBLANES] = 1.0
    return pp, pn


def _inproj_fwd_kernel(xprev_ref, x_ref, xnext_ref, g_ref, w_ref, b_ref, p_ref, pp_ref, pn_ref,
                       cw_ref, cb_ref, wa_ref, ba_ref, wi_ref, bi_ref, lam_ref,
                       q_ref, k_ref, v_ref, sga_ref, sgr_ref, u_ref, gy_ref, hf_ref, carry,
                       *, d_rec, d_model):
    t = pl.program_id(0)
    n_t = pl.num_programs(0)
    rows = SCAN_STEPS * SUBLANES
    prev_rows = (CONV_W // 2) * SUBLANES

    @pl.when(t == 0)
    def _():
        carry[...] = jnp.zeros_like(carry)

    def normed(ref):
        return _rms(ref[...].reshape(-1, d_model), g_ref[...]).astype(BF16)

    def permuted(p, hb):
        return jnp.dot(p[...], hb, preferred_element_type=F32).astype(BF16)

    def proj(lhs, lo, width):
        return jnp.dot(lhs, w_ref[:, lo:lo + width], preferred_element_type=F32) + b_ref[:, lo:lo + width]

    h = normed(x_ref)
    hp = permuted(p_ref, h)
    lhs_u = jnp.concatenate([permuted(pp_ref, normed(xprev_ref)), hp, permuted(pn_ref, normed(xnext_ref))], axis=0)
    lo_u = 3 * D_ATT
    lo_y = lo_u + d_rec
    lo_ga = lo_y + d_rec
    lo_gr = lo_ga + d_model

    def rec_tile(kt):
        cs = slice(kt * MXU_TILE, (kt + 1) * MXU_TILE)
        ucat = proj(lhs_u, lo_u + kt * MXU_TILE, MXU_TILE)
        u_ref[:, cs] = ucat[prev_rows:prev_rows + rows].astype(BF16)
        ucat = jnp.concatenate([jnp.where(t == 0, 0.0, ucat[:prev_rows]), ucat[prev_rows:prev_rows + rows],
                                jnp.where(t == n_t - 1, 0.0, ucat[prev_rows + rows:])], axis=0)
        h_fwd = _rglru_tile(ucat, kt, cw_ref, cb_ref, wa_ref, ba_ref, wi_ref, bi_ref, lam_ref, carry, reverse=False)
        hf_ref[:, cs] = h_fwd.astype(BF16)

    def att_tile(ref, lo, kt, scale):
        val = proj(h, lo + kt * PROJ_TILE, PROJ_TILE)
        _store_batches(ref, val if scale is None else val * scale, D_ATT, kt * PROJ_TILE)

    def gy_tile(kt):
        cs = slice(kt * PROJ_TILE, (kt + 1) * PROJ_TILE)
        gy_ref[:, cs] = _gelu_tanh(proj(hp, lo_y + kt * PROJ_TILE, PROJ_TILE)).astype(BF16)

    def gate_tile(ref, lo, kt):
        _store_batches(ref, _sigmoid(proj(h, lo + kt * PROJ_TILE, PROJ_TILE)), d_model, kt * PROJ_TILE)

    fillers = [functools.partial(att_tile, q_ref, 0, kt, HEAD_DIM ** -0.5) for kt in range(D_ATT // PROJ_TILE)]
    fillers += [functools.partial(att_tile, k_ref, D_ATT, kt, None) for kt in range(D_ATT // PROJ_TILE)]
    fillers += [functools.partial(att_tile, v_ref, 2 * D_ATT, kt, None) for kt in range(D_ATT // PROJ_TILE)]
    fillers += [functools.partial(gy_tile, kt) for kt in range(d_rec // PROJ_TILE)]
    fillers += [functools.partial(gate_tile, sga_ref, lo_ga, kt) for kt in range(d_model // PROJ_TILE)]
    fillers += [functools.partial(gate_tile, sgr_ref, lo_gr, kt) for kt in range(d_model // PROJ_TILE)]
    n_rec = d_rec // MXU_TILE
    per_rec = -(-len(fillers) // n_rec)
    for kt in range(n_rec):
        rec_tile(kt)
        for f in fillers[kt * per_rec:(kt + 1) * per_rec]:
            f()


def _inproj_fwd(x, g, w, b, perms, rec_params, d_rec):
    bsz, s, d = x.shape
    rows = SCAN_STEPS * SUBLANES
    halo_blocks = SCAN_STEPS // SUBLANES
    n_halo = s // SUBLANES

    def tm(width):
        return jax.ShapeDtypeStruct((s, bsz * width), BF16), pl.BlockSpec((SCAN_STEPS, bsz * width), lambda t: (t, 0))

    def sm(width):
        return jax.ShapeDtypeStruct((s * bsz, width), BF16), pl.BlockSpec((rows, width), lambda t: (t, 0))

    outs = [tm(D_ATT), tm(D_ATT), tm(D_ATT), tm(d), tm(d), sm(d_rec), sm(d_rec), sm(d_rec)]
    consts = [g, w, b, *perms, *rec_params]
    return pl.pallas_call(
        functools.partial(_inproj_fwd_kernel, d_rec=d_rec, d_model=d),
        grid=(s // SCAN_STEPS,),
        in_specs=[
            pl.BlockSpec((bsz, SUBLANES, d), lambda t: (0, jnp.maximum(t * halo_blocks - 1, 0), 0)),
            pl.BlockSpec((bsz, SCAN_STEPS, d), lambda t: (0, t, 0)),
            pl.BlockSpec((bsz, SUBLANES, d), lambda t: (0, jnp.minimum((t + 1) * halo_blocks, n_halo - 1), 0)),
        ] + [_const_spec(c.shape) for c in consts],
        out_specs=[o[1] for o in outs],
        out_shape=[o[0] for o in outs],
        scratch_shapes=[pltpu.VMEM((SUBLANES, d_rec), F32)],
        compiler_params=pltpu.CompilerParams(dimension_semantics=("arbitrary",), vmem_limit_bytes=VMEM_LIMIT),
        name="inproj_rglru_fwd",
    )(x, x, x, *consts)


def _attn_kernel(q_ref, k0_ref, k1_ref, k2_ref, v0_ref, v1_ref, v2_ref, tlo_ref, thi_ref, o_ref, bias_ref,
                 *, n_rows):
    band = pl.program_id(0)
    rows_q = ATT_BAND * GRID_W
    lane_row = lax.broadcasted_iota(jnp.int32, (GRID_W, LANES), 1) < HEAD_DIM

    @pl.when(pl.program_id(1) == 0)
    def _():
        r0 = band * ATT_BAND
        k0 = jnp.clip(r0 - WIN_H // 2, 0, n_rows - ATT_KEY_ROWS)
        for i in range(ATT_BAND):
            rq = r0 + i
            rs = jnp.clip(rq - WIN_H // 2, 0, n_rows - WIN_H)
            for jp in range(ATT_KEY_ROWS // HEADS_PER_VREG):
                rka = k0 + 2 * jp
                rkb = rka + 1
                va = jnp.logical_and(rka >= rs, rka < rs + WIN_H)
                vb = jnp.logical_and(rkb >= rs, rkb < rs + WIN_H)
                dra = jnp.clip(rka - rq + (WIN_H - 1), 0, 2 * WIN_H - 2)
                drb = jnp.clip(rkb - rq + (WIN_H - 1), 0, 2 * WIN_H - 2)
                for h in range(N_HEADS):
                    a = jnp.where(va, tlo_ref[h, dra], NEG)
                    b = jnp.where(vb, thi_ref[h, drb], NEG)
                    bias_ref[h, i * GRID_W:(i + 1) * GRID_W, jp * LANES:(jp + 1) * LANES] = jnp.where(lane_row, a, b)

    first = lax.broadcasted_iota(jnp.int32, (rows_q, LANES), 1) < HEAD_DIM
    k_refs = (k0_ref, k1_ref, k2_ref)
    v_refs = (v0_ref, v1_ref, v2_ref)
    zero = jnp.zeros((), BF16)
    for j in range(N_HEADS // HEADS_PER_VREG):
        cs = slice(j * LANES, (j + 1) * LANES)
        q2 = q_ref[:, cs]
        ks = [r[:, cs] for r in k_refs]
        vs = [r[:, cs] for r in v_refs]
        outs = []
        for half in range(HEADS_PER_VREG):
            qh = jnp.where(first if half == 0 else jnp.logical_not(first), q2, zero)
            s = jnp.concatenate(
                [lax.dot_general(qh, kk, (((1,), (1,)), ((), ())), preferred_element_type=F32) for kk in ks],
                axis=1)
            s = s + bias_ref[HEADS_PER_VREG * j + half]
            m = jnp.max(s, axis=-1, keepdims=True)
            p = jnp.exp(s - m)
            l = jnp.sum(p, axis=-1, keepdims=True)
            pb = p.astype(BF16)
            o = jnp.dot(pb[:, 0:MXU_TILE], vs[0], preferred_element_type=F32)
            for jj in range(1, len(vs)):
                o = o + jnp.dot(pb[:, jj * MXU_TILE:(jj + 1) * MXU_TILE], vs[jj], preferred_element_type=F32)
            outs.append(o / l)
        o_ref[:, cs] = jnp.where(first, outs[0], outs[1]).astype(BF16)


def _attention(q, k, v, tlo, thi, bsz):
    s = q.shape[0]
    n_rows = s // GRID_W
    n_bands = n_rows // ATT_BAND
    rows_q = ATT_BAND * GRID_W
    n_kblk = ATT_KEY_ROWS * GRID_W // MXU_TILE
    assert rows_q == MXU_TILE and ATT_KEY_ROWS * GRID_W == n_kblk * MXU_TILE

    def kv_spec(jj):
        def idx(band, bi):
            return (jnp.clip(band - 1, 0, n_bands - n_kblk) + jj, bi)
        return pl.BlockSpec((MXU_TILE, D_ATT), idx)

    return pl.pallas_call(
        functools.partial(_attn_kernel, n_rows=n_rows),
        grid=(n_bands, bsz),
        in_specs=[pl.BlockSpec((rows_q, D_ATT), lambda band, bi: (band, bi))]
        + [kv_spec(jj) for jj in range(n_kblk)] * 2
        + [_const_spec(tlo.shape), _const_spec(thi.shape)],
        out_specs=pl.BlockSpec((rows_q, D_ATT), lambda band, bi: (band, bi)),
        out_shape=jax.ShapeDtypeStruct((s, bsz * D_ATT), BF16),
        scratch_shapes=[pltpu.VMEM((N_HEADS, rows_q, ATT_KEY_ROWS * GRID_W), F32)],
        compiler_params=pltpu.CompilerParams(
            dimension_semantics=("arbitrary", "arbitrary"), vmem_limit_bytes=VMEM_LIMIT),
        name="nbr_attention",
    )(q, k, k, k, v, v, v, tlo, thi)


def _bias_tables(rpb):
    qc = np.arange(GRID_W)[:, None]
    kc = np.arange(GRID_W)[None, :]
    ws = np.clip(qc - WIN_W // 2, 0, GRID_W - WIN_W)
    valid = (kc >= ws) & (kc < ws + WIN_W)
    d_col = np.clip(kc - qc, -(WIN_W - 1), WIN_W - 1) + (WIN_W - 1)
    onehot = (np.arange(2 * WIN_W - 1)[:, None, None] == d_col[None]).astype(np.float32)
    t = jnp.einsum("hrd,dqk->hrqk", rpb.astype(F32), onehot, precision=lax.Precision.HIGHEST)
    t = jnp.where(valid[None, None], t, NEG)
    neg = jnp.full_like(t, NEG)
    return jnp.concatenate([t, neg], axis=-1), jnp.concatenate([neg, t], axis=-1)


def _rglru_tile(ucat, kt, cw_ref, cb_ref, wa_ref, ba_ref, wi_ref, bi_ref, lam_ref, carry, reverse):
    rows = SCAN_STEPS * SUBLANES
    cs = slice(kt * MXU_TILE, (kt + 1) * MXU_TILE)
    conv = cb_ref[:, cs]
    for j in range(CONV_W):
        conv = conv + ucat[j * SUBLANES:j * SUBLANES + rows] * cw_ref[j:j + 1, cs]
    conv_b = conv.astype(BF16)

    def gate(w_ref, b_ref):
        return _sigmoid(jnp.dot(conv_b, w_ref[kt], preferred_element_type=F32) + b_ref[:, cs])

    r_gate = gate(wa_ref, ba_ref)
    i_gate = gate(wi_ref, bi_ref)
    log_a = r_gate * (-LRU_C * _softplus(-lam_ref[:, cs]))
    a = jnp.exp(log_a)
    z = jnp.maximum(jnp.tanh(log_a) * (-1.0 - a * a), 0.0)
    mult = jnp.where(z > 0.0, z * lax.rsqrt(z), 0.0)
    bx = mult * (i_gate * conv)

    h = carry[:, cs]
    hs = [None] * SCAN_STEPS
    for i in range(SCAN_STEPS):
        t = SCAN_STEPS - 1 - i if reverse else i
        sl = slice(t * SUBLANES, (t + 1) * SUBLANES)
        h = a[sl] * h + bx[sl]
        hs[t] = h
    carry[:, cs] = h
    return jnp.concatenate(hs, axis=0)


def _rglru_bwd_merge_kernel(uprev_ref, u_ref, unext_ref, cw_ref, cb_ref, wa_ref, ba_ref, wi_ref, bi_ref, lam_ref,
                            gy_ref, hf_ref, pt_ref, x_ref, att_ref, sga_ref, sgr_ref, wao_ref, wro_ref, wout_ref,
                            o_ref, carry, y_s):
    j = pl.program_id(0)
    n_t = pl.num_programs(0) - 1
    rows = SCAN_STEPS * SUBLANES
    d = x_ref.shape[-1]
    prev_rows = (CONV_W // 2) * SUBLANES

    @pl.when(j == 0)
    def _():
        carry[...] = jnp.zeros_like(carry)
        y_s[...] = jnp.zeros_like(y_s)

    t_idx = n_t - 1 - jnp.minimum(j, n_t - 1)
    y_nat = jnp.dot(pt_ref[...], y_s[...], preferred_element_type=F32).astype(BF16)
    att_nat = _stack_batches(att_ref, D_ATT)
    out = x_ref[...].reshape(rows, d)
    for kt in range(d // MXU_TILE):
        cs = slice(kt * MXU_TILE, (kt + 1) * MXU_TILE)
        y_rec = jnp.dot(y_nat, wro_ref[:, cs], preferred_element_type=F32)
        y_att = jnp.dot(att_nat, wao_ref[:, cs], preferred_element_type=F32)
        mixed = (_stack_batches(sga_ref, d, kt * MXU_TILE, MXU_TILE).astype(F32) * y_att
                 + _stack_batches(sgr_ref, d, kt * MXU_TILE, MXU_TILE).astype(F32) * y_rec)
        out = out + jnp.dot(mixed.astype(BF16), wout_ref[cs, :], preferred_element_type=F32)

        prev = jnp.where(t_idx == 0, 0.0, uprev_ref[:, cs].astype(F32)[BF16_ROWS - prev_rows:])
        nxt = jnp.where(t_idx == n_t - 1, 0.0, unext_ref[:, cs].astype(F32)[0:SUBLANES])
        ucat = jnp.concatenate([prev, u_ref[:, cs].astype(F32), nxt], axis=0)
        h_bwd = _rglru_tile(ucat, kt, cw_ref, cb_ref, wa_ref, ba_ref, wi_ref, bi_ref, lam_ref, carry, reverse=True)
        y_s[:, cs] = ((hf_ref[:, cs].astype(F32) + h_bwd) * gy_ref[:, cs].astype(F32)).astype(BF16)
    o_ref[...] = out.reshape(o_ref.shape)


def _rglru_bwd_merge(u, params, gy, h_fwd, perm_t, x, att, sga, sgr, wao, wro, wout):
    n_rows, c = u.shape
    bsz, s, d = x.shape
    rows = SCAN_STEPS * SUBLANES
    n_t = s // SCAN_STEPS
    halo_per_block = rows // BF16_ROWS
    n_halo = n_rows // BF16_ROWS

    def scan_blk(j):
        return n_t - 1 - jnp.minimum(j, n_t - 1)

    def merge_blk(j):
        return n_t - 1 - jnp.maximum(j - 1, 0)

    sm_spec = pl.BlockSpec((rows, c), lambda j: (scan_blk(j), 0))
    halo_specs = [
        pl.BlockSpec((BF16_ROWS, c), lambda j: (jnp.maximum(scan_blk(j) * halo_per_block - 1, 0), 0)),
        sm_spec,
        pl.BlockSpec((BF16_ROWS, c), lambda j: (jnp.minimum((scan_blk(j) + 1) * halo_per_block, n_halo - 1), 0)),
    ]

    def tm(width):
        return pl.BlockSpec((SCAN_STEPS, bsz * width), lambda j: (merge_blk(j), 0))

    x_spec = pl.BlockSpec((bsz, SCAN_STEPS, d), lambda j: (0, merge_blk(j), 0))
    consts = [perm_t]
    weights = [wao, wro, wout]
    return pl.pallas_call(
        _rglru_bwd_merge_kernel,
        grid=(n_t + 1,),
        in_specs=halo_specs + [_const_spec(p.shape) for p in params] + [sm_spec, sm_spec]
        + [_const_spec(p.shape) for p in consts] + [x_spec, tm(D_ATT), tm(d), tm(d)]
        + [_const_spec(p.shape) for p in weights],
        out_specs=x_spec,
        out_shape=jax.ShapeDtypeStruct(x.shape, F32),
        scratch_shapes=[pltpu.VMEM((SUBLANES, c), F32), pltpu.VMEM((rows, c), BF16)],
        compiler_params=pltpu.CompilerParams(dimension_semantics=("arbitrary",), vmem_limit_bytes=VMEM_LIMIT),
        name="rglru_bwd_merge",
    )(u, u, u, *params, gy, h_fwd, *consts, x, att, sga, sgr, *weights)


def _gate_tiles(w):
    nb, bw, _ = w.shape
    per = MXU_TILE // bw
    eye = jnp.eye(per, dtype=w.dtype)
    t = jnp.einsum("cpij,pq->cpiqj", w.reshape(nb // per, per, bw, bw), eye)
    return t.reshape(nb // per, MXU_TILE, MXU_TILE).astype(BF16)


def _ffn_kernel(x_ref, g2_ref, w1_ref, w2_ref, gf_ref, o_ref, *, ff_chunk):
    x = x_ref[...]
    h = _rms(x, g2_ref[...]).astype(BF16)
    d_ff = w1_ref.shape[1]
    acc = x
    for c in range(d_ff // ff_chunk):
        cs = slice(c * ff_chunk, (c + 1) * ff_chunk)
        hid = jnp.maximum(jnp.dot(h, w1_ref[:, cs], preferred_element_type=F32), 0.0)
        acc = acc + jnp.dot((hid * hid).astype(BF16), w2_ref[cs, :], preferred_element_type=F32)
    o_ref[...] = _rms(acc, gf_ref[...])


def _ffn(x2d, g2, w1, w2, gf):
    n, d = x2d.shape
    spec = pl.BlockSpec((ROW_BLOCK, d), lambda i: (i, 0))
    return pl.pallas_call(
        functools.partial(_ffn_kernel, ff_chunk=1024),
        grid=(n // ROW_BLOCK,),
        in_specs=[spec, _const_spec(g2.shape), _const_spec(w1.shape), _const_spec(w2.shape), _const_spec(gf.shape)],
        out_specs=spec,
        out_shape=jax.ShapeDtypeStruct((n, d), F32),
        compiler_params=pltpu.CompilerParams(dimension_semantics=("parallel",), vmem_limit_bytes=VMEM_LIMIT),
        name="ffn_final_norm",
    )(x2d, g2, w1, w2, gf)


def kernel(x, ln1_g, w_in, b_in, rpb, w_att_o, conv_w, conv_b, w_rg_a, b_rg_a, w_rg_i, b_rg_i, lru_lambda,
           w_rec_o, w_out, ln2_g, w_ff1, w_ff2, lnf_g):
    bsz, s, d = x.shape
    d_rec = conv_w.shape[-1]
    assert ln1_g.shape[0] == 1, "single-layer stack only"
    assert bsz == SUBLANES and s % SCAN_STEPS == 0 and s % (ATT_BAND * GRID_W) == 0 and (bsz * s) % ROW_BLOCK == 0
    assert w_in.shape[-1] == 3 * D_ATT + 2 * d_rec + 2 * d and d_rec % MXU_TILE == 0 and d_rec == d
    row = lambda p: p.reshape(1, -1).astype(F32)
    l = 0

    def dir_params(di):
        return [conv_w[l].astype(F32), row(conv_b[l]), _gate_tiles(w_rg_a[l, di]), row(b_rg_a[l, di]),
                _gate_tiles(w_rg_i[l, di]), row(b_rg_i[l, di]), row(lru_lambda[l, di])]

    perm = _perm_matrix()
    perms = [jnp.asarray(m, BF16) for m in (perm, *_halo_perm_matrices())]
    q, k, v, sga, sgr, u, gy, h_fwd = _inproj_fwd(x, row(ln1_g[l]), w_in[l].astype(BF16), row(b_in[l]), perms,
                                                  dir_params(0), d_rec)

    tlo, thi = _bias_tables(rpb[l])
    att = _attention(q, k, v, tlo, thi, bsz)

    x1 = _rglru_bwd_merge(u, dir_params(1), gy, h_fwd, jnp.asarray(perm.T, BF16), x, att, sga, sgr,
                          w_att_o[l].astype(BF16), w_rec_o[l].astype(BF16), w_out[l].astype(BF16))

    out = _ffn(x1.reshape(bsz * s, d), row(ln2_g[l]), w_ff1[l].astype(BF16), w_ff2[l].astype(BF16), row(lnf_g))
    return out.reshape(bsz, s, d)
```

```python
import functools

import jax
import jax.numpy as jnp
import numpy as np
from jax import lax
from jax.experimental import pallas as pl
from jax.experimental.pallas import tpu as pltpu

F32 = jnp.float32
BF16 = jnp.bfloat16

GRID_W = 64
N_HEADS = 8
HEAD_DIM = 64
D_ATT = N_HEADS * HEAD_DIM
WIN_H = 8
WIN_W = 16
CONV_W = 4
LRU_C = 8.0
EPS = 1e-6
NEG = -1e30
LOG2E = float(np.log2(np.e))

LANES = 128
SUBLANES = 8
BF16_ROWS = 16
MXU_TILE = 256
VMEM_LIMIT = 56 * 1024 * 1024

ROW_BLOCK = 512
PROJ_TILE = 256
SCAN_STEPS = 64
ATT_BAND = 4
ATT_KEY_ROWS = 12
HEADS_PER_VREG = LANES // HEAD_DIM


def _rms(x, g):
    ms = jnp.mean(x * x, axis=-1, keepdims=True)
    return x * lax.rsqrt(ms + EPS) * g


def _sigmoid(x):
    return 1.0 / (1.0 + jnp.exp2(x * (-LOG2E)))


def _gelu_tanh(x):
    c = float(np.sqrt(2.0 / np.pi))
    return 0.5 * x * (1.0 + jnp.tanh(c * (x + 0.044715 * (x * x * x))))


def _softplus(x):
    return jnp.maximum(x, 0.0) + jnp.log1p(jnp.exp(-jnp.abs(x)))


def _const_spec(shape):
    nd = len(shape)
    return pl.BlockSpec(shape, lambda *_: (0,) * nd, pipeline_mode=pl.Buffered(1))


def _stack_batches(ref, c, lo=0, width=None):
    width = c if width is None else width
    return jnp.concatenate([ref[:, b * c + lo:b * c + lo + width] for b in range(SUBLANES)], axis=0)


def _store_batches(ref, val, c, lo=0):
    steps = ref.shape[0]
    width = val.shape[1]
    for b in range(SUBLANES):
        ref[:, b * c + lo:b * c + lo + width] = val[b * steps:(b + 1) * steps].astype(ref.dtype)


def _perm_matrix():
    rows = SCAN_STEPS * SUBLANES
    p = np.zeros((rows, rows), np.float32)
    t, b = np.meshgrid(np.arange(SCAN_STEPS), np.arange(SUBLANES), indexing="ij")
    p[(t * SUBLANES + b).ravel(), (b * SCAN_STEPS + t).ravel()] = 1.0
    return p


def _halo_perm_matrices():
    prev_steps = CONV_W // 2
    halo_rows = SUBLANES * SUBLANES
    pp = np.zeros((prev_steps * SUBLANES, halo_rows), np.float32)
    pn = np.zeros((SUBLANES, halo_rows), np.float32)
    for b in range(SUBLANES):
        for j in range(prev_steps):
            pp[j * SUBLANES + b, b * SUBLANES + SUBLANES - prev_steps + j] = 1.0
        pn[b, b * SUBLANES] = 1.0
    return pp, pn


def _inproj_fwd_kernel(xprev_ref, x_ref, xnext_ref, g_ref, w_ref, b_ref, p_ref, pp_ref, pn_ref,
                       cw_ref, cb_ref, wa_ref, ba_ref, wi_ref, bi_ref, lam_ref,
                       q_ref, k_ref, v_ref, sga_ref, sgr_ref, u_ref, gy_ref, hf_ref, carry,
                       *, d_rec, d_model):
    t = pl.program_id(0)
    n_t = pl.num_programs(0)
    rows = SCAN_STEPS * SUBLANES
    prev_rows = (CONV_W // 2) * SUBLANES

    @pl.when(t == 0)
    def _():
        carry[...] = jnp.zeros_like(carry)

    def normed(ref):
        return _rms(ref[...].reshape(-1, d_model), g_ref[...]).astype(BF16)

    def permuted(p, hb):
        return jnp.dot(p[...], hb, preferred_element_type=F32).astype(BF16)

    def proj(lhs, lo, width):
        return jnp.dot(lhs, w_ref[:, lo:lo + width], preferred_element_type=F32) + b_ref[:, lo:lo + width]

    h = normed(x_ref)
    hp = permuted(p_ref, h)
    lhs_u = jnp.concatenate([permuted(pp_ref, normed(xprev_ref)), hp, permuted(pn_ref, normed(xnext_ref))], axis=0)
    lo_u = 3 * D_ATT
    lo_y = lo_u + d_rec
    lo_ga = lo_y + d_rec
    lo_gr = lo_ga + d_model

    def rec_tile(kt):
        cs = slice(kt * MXU_TILE, (kt + 1) * MXU_TILE)
        ucat = proj(lhs_u, lo_u + kt * MXU_TILE, MXU_TILE)
        u_ref[:, cs] = ucat[prev_rows:prev_rows + rows].astype(BF16)
        ucat = jnp.concatenate([jnp.where(t == 0, 0.0, ucat[:prev_rows]), ucat[prev_rows:prev_rows + rows],
                                jnp.where(t == n_t - 1, 0.0, ucat[prev_rows + rows:])], axis=0)
        h_fwd = _rglru_tile(ucat, kt, cw_ref, cb_ref, wa_ref, ba_ref, wi_ref, bi_ref, lam_ref, carry, reverse=False)
        hf_ref[:, cs] = h_fwd.astype(BF16)

    def att_tile(ref, lo, kt, scale):
        val = proj(h, lo + kt * PROJ_TILE, PROJ_TILE)
        _store_batches(ref, val if scale is None else val * scale, D_ATT, kt * PROJ_TILE)

    def gy_tile(kt):
        cs = slice(kt * PROJ_TILE, (kt + 1) * PROJ_TILE)
        gy_ref[:, cs] = _gelu_tanh(proj(hp, lo_y + kt * PROJ_TILE, PROJ_TILE)).astype(BF16)

    def gate_tile(ref, lo, kt):
        _store_batches(ref, _sigmoid(proj(h, lo + kt * PROJ_TILE, PROJ_TILE)), d_model, kt * PROJ_TILE)

    fillers = [functools.partial(att_tile, q_ref, 0, kt, HEAD_DIM ** -0.5) for kt in range(D_ATT // PROJ_TILE)]
    fillers += [functools.partial(att_tile, k_ref, D_ATT, kt, None) for kt in range(D_ATT // PROJ_TILE)]
    fillers += [functools.partial(att_tile, v_ref, 2 * D_ATT, kt, None) for kt in range(D_ATT // PROJ_TILE)]
    fillers += [functools.partial(gy_tile, kt) for kt in range(d_rec // PROJ_TILE)]
    fillers += [functools.partial(gate_tile, sga_ref, lo_ga, kt) for kt in range(d_model // PROJ_TILE)]
    fillers += [functools.partial(gate_tile, sgr_ref, lo_gr, kt) for kt in range(d_model // PROJ_TILE)]
    n_rec = d_rec // MXU_TILE
    per_rec = -(-len(fillers) // n_rec)
    for kt in range(n_rec):
        rec_tile(kt)
        for f in fillers[kt * per_rec:(kt + 1) * per_rec]:
            f()


def _inproj_fwd(x, g, w, b, perms, rec_params, d_rec):
    bsz, s, d = x.shape
    rows = SCAN_STEPS * SUBLANES
    halo_blocks = SCAN_STEPS // SUBLANES
    n_halo = s // SUBLANES

    def tm(width):
        return jax.ShapeDtypeStruct((s, bsz * width), BF16), pl.BlockSpec((SCAN_STEPS, bsz * width), lambda t: (t, 0))

    def sm(width):
        return jax.ShapeDtypeStruct((s * bsz, width), BF16), pl.BlockSpec((rows, width), lambda t: (t, 0))

    outs = [tm(D_ATT), tm(D_ATT), tm(D_ATT), tm(d), tm(d), sm(d_rec), sm(d_rec), sm(d_rec)]
    consts = [g, w, b, *perms, *rec_params]
    return pl.pallas_call(
        functools.partial(_inproj_fwd_kernel, d_rec=d_rec, d_model=d),
        grid=(s // SCAN_STEPS,),
        in_specs=[
            pl.BlockSpec((bsz, SUBLANES, d), lambda t: (0, jnp.maximum(t * halo_blocks - 1, 0), 0)),
            pl.BlockSpec((bsz, SCAN_STEPS, d), lambda t: (0, t, 0)),
            pl.BlockSpec((bsz, SUBLANES, d), lambda t: (0, jnp.minimum((t + 1) * halo_blocks, n_halo - 1), 0)),
        ] + [_const_spec(c.shape) for c in consts],
        out_specs=[o[1] for o in outs],
        out_shape=[o[0] for o in outs],
        scratch_shapes=[pltpu.VMEM((SUBLANES, d_rec), F32)],
        compiler_params=pltpu.CompilerParams(dimension_semantics=("arbitrary",), vmem_limit_bytes=VMEM_LIMIT),
        name="inproj_rglru_fwd",
    )(x, x, x, *consts)


def _attn_kernel(q_ref, k0_ref, k1_ref, k2_ref, v0_ref, v1_ref, v2_ref, tlo_ref, thi_ref, o_ref, bias_ref,
                 *, n_rows):
    band = pl.program_id(0)
    rows_q = ATT_BAND * GRID_W
    lane_row = lax.broadcasted_iota(jnp.int32, (GRID_W, LANES), 1) < HEAD_DIM

    @pl.when(pl.program_id(1) == 0)
    def _():
        r0 = band * ATT_BAND
        k0 = jnp.clip(r0 - WIN_H // 2, 0, n_rows - ATT_KEY_ROWS)
        for i in range(ATT_BAND):
            rq = r0 + i
            rs = jnp.clip(rq - WIN_H // 2, 0, n_rows - WIN_H)
            for jp in range(ATT_KEY_ROWS // HEADS_PER_VREG):
                rka = k0 + 2 * jp
                rkb = rka + 1
                va = jnp.logical_and(rka >= rs, rka < rs + WIN_H)
                vb = jnp.logical_and(rkb >= rs, rkb < rs + WIN_H)
                dra = jnp.clip(rka - rq + (WIN_H - 1), 0, 2 * WIN_H - 2)
                drb = jnp.clip(rkb - rq + (WIN_H - 1), 0, 2 * WIN_H - 2)
                for h in range(N_HEADS):
                    a = jnp.where(va, tlo_ref[h, dra], NEG)
                    b = jnp.where(vb, thi_ref[h, drb], NEG)
                    bias_ref[h, i * GRID_W:(i + 1) * GRID_W, jp * LANES:(jp + 1) * LANES] = jnp.where(lane_row, a, b)

    first = lax.broadcasted_iota(jnp.int32, (rows_q, LANES), 1) < HEAD_DIM
    k_refs = (k0_ref, k1_ref, k2_ref)
    v_refs = (v0_ref, v1_ref, v2_ref)
    zero = jnp.zeros((), BF16)

    def scores(hd):
        j, half = divmod(hd, HEADS_PER_VREG)
        cs = slice(j * LANES, (j + 1) * LANES)
        qh = jnp.where(first if half == 0 else jnp.logical_not(first), q_ref[:, cs], zero)
        s = jnp.concatenate(
            [lax.dot_general(qh, r[:, cs], (((1,), (1,)), ((), ())), preferred_element_type=F32) for r in k_refs],
            axis=1) + bias_ref[hd]
        return s, jnp.max(s, axis=-1, keepdims=True)

    nxt = scores(0)
    outs = []
    for hd in range(N_HEADS):
        j = hd // HEADS_PER_VREG
        cs = slice(j * LANES, (j + 1) * LANES)
        s, m = nxt
        if hd + 1 < N_HEADS:
            nxt = scores(hd + 1)
        p = jnp.exp(s - m)
        l = jnp.sum(p, axis=-1, keepdims=True)
        pb = p.astype(BF16)
        o = jnp.dot(pb[:, 0:MXU_TILE], v_refs[0][:, cs], preferred_element_type=F32)
        for jj in range(1, len(v_refs)):
            o = o + jnp.dot(pb[:, jj * MXU_TILE:(jj + 1) * MXU_TILE], v_refs[jj][:, cs], preferred_element_type=F32)
        outs.append(o / l)
        if len(outs) == HEADS_PER_VREG:
            o_ref[:, cs] = jnp.where(first, outs[0], outs[1]).astype(BF16)
            outs = []


def _attention(q, k, v, tlo, thi, bsz):
    s = q.shape[0]
    n_rows = s // GRID_W
    n_bands = n_rows // ATT_BAND
    rows_q = ATT_BAND * GRID_W
    n_kblk = ATT_KEY_ROWS * GRID_W // MXU_TILE
    assert rows_q == MXU_TILE and ATT_KEY_ROWS * GRID_W == n_kblk * MXU_TILE

    def kv_spec(jj):
        def idx(band, bi):
            return (jnp.clip(band - 1, 0, n_bands - n_kblk) + jj, bi)
        return pl.BlockSpec((MXU_TILE, D_ATT), idx)

    return pl.pallas_call(
        functools.partial(_attn_kernel, n_rows=n_rows),
        grid=(n_bands, bsz),
        in_specs=[pl.BlockSpec((rows_q, D_ATT), lambda band, bi: (band, bi))]
        + [kv_spec(jj) for jj in range(n_kblk)] * 2
        + [_const_spec(tlo.shape), _const_spec(thi.shape)],
        out_specs=pl.BlockSpec((rows_q, D_ATT), lambda band, bi: (band, bi)),
        out_shape=jax.ShapeDtypeStruct((s, bsz * D_ATT), BF16),
        scratch_shapes=[pltpu.VMEM((N_HEADS, rows_q, ATT_KEY_ROWS * GRID_W), F32)],
        compiler_params=pltpu.CompilerParams(
            dimension_semantics=("arbitrary", "arbitrary"), vmem_limit_bytes=VMEM_LIMIT),
        name="nbr_attention",
    )(q, k, k, k, v, v, v, tlo, thi)


def _bias_tables(rpb):
    qc = np.arange(GRID_W)[:, None]
    kc = np.arange(GRID_W)[None, :]
    ws = np.clip(qc - WIN_W // 2, 0, GRID_W - WIN_W)
    valid = (kc >= ws) & (kc < ws + WIN_W)
    d_col = np.clip(kc - qc, -(WIN_W - 1), WIN_W - 1) + (WIN_W - 1)
    onehot = (np.arange(2 * WIN_W - 1)[:, None, None] == d_col[None]).astype(np.float32)
    t = jnp.einsum("hrd,dqk->hrqk", rpb.astype(F32), onehot, precision=lax.Precision.HIGHEST)
    t = jnp.where(valid[None, None], t, NEG)
    neg = jnp.full_like(t, NEG)
    return jnp.concatenate([t, neg], axis=-1), jnp.concatenate([neg, t], axis=-1)


def _rglru_tile(ucat, kt, cw_ref, cb_ref, wa_ref, ba_ref, wi_ref, bi_ref, lam_ref, carry, reverse):
    rows = SCAN_STEPS * SUBLANES
    cs = slice(kt * MXU_TILE, (kt + 1) * MXU_TILE)
    conv = cb_ref[:, cs]
    for j in range(CONV_W):
        conv = conv + ucat[j * SUBLANES:j * SUBLANES + rows] * cw_ref[j:j + 1, cs]
    conv_b = conv.astype(BF16)

    def gate(w_ref, b_ref):
        return _sigmoid(jnp.dot(conv_b, w_ref[kt], preferred_element_type=F32) + b_ref[:, cs])

    r_gate = gate(wa_ref, ba_ref)
    i_gate = gate(wi_ref, bi_ref)
    log_a = r_gate * (-LRU_C * _softplus(-lam_ref[:, cs]))
    a = jnp.exp(log_a)
    z = jnp.maximum(jnp.tanh(log_a) * (-1.0 - a * a), 0.0)
    mult = jnp.where(z > 0.0, z * lax.rsqrt(z), 0.0)
    bx = mult * (i_gate * conv)

    h = carry[:, cs]
    hs = [None] * SCAN_STEPS
    for i in range(SCAN_STEPS):
        t = SCAN_STEPS - 1 - i if reverse else i
        sl = slice(t * SUBLANES, (t + 1) * SUBLANES)
        h = a[sl] * h + bx[sl]
        hs[t] = h
    carry[:, cs] = h
    return jnp.concatenate(hs, axis=0)


def _rglru_bwd_merge_kernel(uprev_ref, u_ref, unext_ref, cw_ref, cb_ref, wa_ref, ba_ref, wi_ref, bi_ref, lam_ref,
                            gy_ref, hf_ref, pt_ref, x_ref, att_ref, sga_ref, sgr_ref, wao_ref, wro_ref, wout_ref,
                            o_ref, carry, y_s):
    j = pl.program_id(0)
    n_t = pl.num_programs(0) - 1
    rows = SCAN_STEPS * SUBLANES
    d = x_ref.shape[-1]
    prev_rows = (CONV_W // 2) * SUBLANES

    @pl.when(j == 0)
    def _():
        carry[...] = jnp.zeros_like(carry)
        y_s[...] = jnp.zeros_like(y_s)

    t_idx = n_t - 1 - jnp.minimum(j, n_t - 1)
    y_nat = jnp.dot(pt_ref[...], y_s[...], preferred_element_type=F32).astype(BF16)
    att_nat = _stack_batches(att_ref, D_ATT)
    out = x_ref[...].reshape(rows, d)
    for kt in range(d // MXU_TILE):
        cs = slice(kt * MXU_TILE, (kt + 1) * MXU_TILE)
        y_rec = jnp.dot(y_nat, wro_ref[:, cs], preferred_element_type=F32)
        y_att = jnp.dot(att_nat, wao_ref[:, cs], preferred_element_type=F32)
        mixed = (_stack_batches(sga_ref, d, kt * MXU_TILE, MXU_TILE).astype(F32) * y_att
                 + _stack_batches(sgr_ref, d, kt * MXU_TILE, MXU_TILE).astype(F32) * y_rec)
        out = out + jnp.dot(mixed.astype(BF16), wout_ref[cs, :], preferred_element_type=F32)

        prev = jnp.where(t_idx == 0, 0.0, uprev_ref[:, cs].astype(F32)[BF16_ROWS - prev_rows:])
        nxt = jnp.where(t_idx == n_t - 1, 0.0, unext_ref[:, cs].astype(F32)[0:SUBLANES])
        ucat = jnp.concatenate([prev, u_ref[:, cs].astype(F32), nxt], axis=0)
        h_bwd = _rglru_tile(ucat, kt, cw_ref, cb_ref, wa_ref, ba_ref, wi_ref, bi_ref, lam_ref, carry, reverse=True)
        y_s[:, cs] = ((hf_ref[:, cs].astype(F32) + h_bwd) * gy_ref[:, cs].astype(F32)).astype(BF16)
    o_ref[...] = out.reshape(o_ref.shape)


def _rglru_bwd_merge(u, params, gy, h_fwd, perm_t, x, att, sga, sgr, wao, wro, wout):
    n_rows, c = u.shape
    bsz, s, d = x.shape
    rows = SCAN_STEPS * SUBLANES
    n_t = s // SCAN_STEPS
    halo_per_block = rows // BF16_ROWS
    n_halo = n_rows // BF16_ROWS

    def scan_blk(j):
        return n_t - 1 - jnp.minimum(j, n_t - 1)

    def merge_blk(j):
        return n_t - 1 - jnp.maximum(j - 1, 0)

    sm_spec = pl.BlockSpec((rows, c), lambda j: (scan_blk(j), 0))
    halo_specs = [
        pl.BlockSpec((BF16_ROWS, c), lambda j: (jnp.maximum(scan_blk(j) * halo_per_block - 1, 0), 0)),
        sm_spec,
        pl.BlockSpec((BF16_ROWS, c), lambda j: (jnp.minimum((scan_blk(j) + 1) * halo_per_block, n_halo - 1), 0)),
    ]

    def tm(width):
        return pl.BlockSpec((SCAN_STEPS, bsz * width), lambda j: (merge_blk(j), 0))

    x_spec = pl.BlockSpec((bsz, SCAN_STEPS, d), lambda j: (0, merge_blk(j), 0))
    consts = [perm_t]
    weights = [wao, wro, wout]
    return pl.pallas_call(
        _rglru_bwd_merge_kernel,
        grid=(n_t + 1,),
        in_specs=halo_specs + [_const_spec(p.shape) for p in params] + [sm_spec, sm_spec]
        + [_const_spec(p.shape) for p in consts] + [x_spec, tm(D_ATT), tm(d), tm(d)]
        + [_const_spec(p.shape) for p in weights],
        out_specs=x_spec,
        out_shape=jax.ShapeDtypeStruct(x.shape, F32),
        scratch_shapes=[pltpu.VMEM((SUBLANES, c), F32), pltpu.VMEM((rows, c), BF16)],
        compiler_params=pltpu.CompilerParams(dimension_semantics=("arbitrary",), vmem_limit_bytes=VMEM_LIMIT),
        name="rglru_bwd_merge",
    )(u, u, u, *params, gy, h_fwd, *consts, x, att, sga, sgr, *weights)


def _gate_tiles(w):
    nb, bw, _ = w.shape
    per = MXU_TILE // bw
    eye = jnp.eye(per, dtype=w.dtype)
    t = jnp.einsum("cpij,pq->cpiqj", w.reshape(nb // per, per, bw, bw), eye)
    return t.reshape(nb // per, MXU_TILE, MXU_TILE).astype(BF16)


def _ffn_kernel(x_ref, g2_ref, w1_ref, w2_ref, gf_ref, o_ref, *, ff_chunk):
    x = x_ref[...]
    h = _rms(x, g2_ref[...]).astype(BF16)
    d_ff = w1_ref.shape[1]
    acc = x
    for c in range(d_ff // ff_chunk):
        cs = slice(c * ff_chunk, (c + 1) * ff_chunk)
        hid = jnp.maximum(jnp.dot(h, w1_ref[:, cs], preferred_element_type=F32), 0.0)
        acc = acc + jnp.dot((hid * hid).astype(BF16), w2_ref[cs, :], preferred_element_type=F32)
    o_ref[...] = _rms(acc, gf_ref[...])


def _ffn(x2d, g2, w1, w2, gf):
    n, d = x2d.shape
    spec = pl.BlockSpec((ROW_BLOCK, d), lambda i: (i, 0))
    return pl.pallas_call(
        functools.partial(_ffn_kernel, ff_chunk=1024),
        grid=(n // ROW_BLOCK,),
        in_specs=[spec, _const_spec(g2.shape), _const_spec(w1.shape), _const_spec(w2.shape), _const_spec(gf.shape)],
        out_specs=spec,
        out_shape=jax.ShapeDtypeStruct((n, d), F32),
        compiler_params=pltpu.CompilerParams(dimension_semantics=("parallel",), vmem_limit_bytes=VMEM_LIMIT),
        name="ffn_final_norm",
    )(x2d, g2, w1, w2, gf)


def kernel(x, ln1_g, w_in, b_in, rpb, w_att_o, conv_w, conv_b, w_rg_a, b_rg_a, w_rg_i, b_rg_i, lru_lambda,
           w_rec_o, w_out, ln2_g, w_ff1, w_ff2, lnf_g):
    bsz, s, d = x.shape
    d_rec = conv_w.shape[-1]
    assert ln1_g.shape[0] == 1, "single-layer stack only"
    assert bsz == SUBLANES and s % SCAN_STEPS == 0 and s % (ATT_BAND * GRID_W) == 0 and (bsz * s) % ROW_BLOCK == 0
    assert w_in.shape[-1] == 3 * D_ATT + 2 * d_rec + 2 * d and d_rec % MXU_TILE == 0 and d_rec == d
    row = lambda p: p.reshape(1, -1).astype(F32)
    l = 0

    def dir_params(di):
        return [conv_w[l].astype(F32), row(conv_b[l]), _gate_tiles(w_rg_a[l, di]), row(b_rg_a[l, di]),
                _gate_tiles(w_rg_i[l, di]), row(b_rg_i[l, di]), row(lru_lambda[l, di])]

    perm = _perm_matrix()
    perms = [jnp.asarray(m, BF16) for m in (perm, *_halo_perm_matrices())]
    q, k, v, sga, sgr, u, gy, h_fwd = _inproj_fwd(x, row(ln1_g[l]), w_in[l].astype(BF16), row(b_in[l]), perms,
                                                  dir_params(0), d_rec)

    tlo, thi = _bias_tables(rpb[l])
    att = _attention(q, k, v, tlo, thi, bsz)

    x1 = _rglru_bwd_merge(u, dir_params(1), gy, h_fwd, jnp.asarray(perm.T, BF16), x, att, sga, sgr,
                          w_att_o[l].astype(BF16), w_rec_o[l].astype(BF16), w_out[l].astype(BF16))

    out = _ffn(x1.reshape(bsz * s, d), row(ln2_g[l]), w_ff1[l].astype(BF16), w_ff2[l].astype(BF16), row(lnf_g))
    return out.reshape(bsz, s, d)
```

```python
import functools

import jax
import jax.numpy as jnp
import numpy as np
from jax import lax
from jax.experimental import pallas as pl
from jax.experimental.pallas import tpu as pltpu

F32 = jnp.float32
BF16 = jnp.bfloat16

GRID_W = 64
N_HEADS = 8
HEAD_DIM = 64
D_ATT = N_HEADS * HEAD_DIM
WIN_H = 8
WIN_W = 16
CONV_W = 4
LRU_C = 8.0
EPS = 1e-6
NEG = -1e30
LOG2E = float(np.log2(np.e))

LANES = 128
SUBLANES = 8
BF16_ROWS = 16
MXU_TILE = 256
VMEM_LIMIT = 56 * 1024 * 1024

ROW_BLOCK = 512
PROJ_TILE = 256
SCAN_STEPS = 64
ATT_BAND = 4
ATT_KEY_ROWS = 12
HEADS_PER_VREG = LANES // HEAD_DIM


def _rms(x, g):
    ms = jnp.mean(x * x, axis=-1, keepdims=True)
    return x * lax.rsqrt(ms + EPS) * g


def _sigmoid(x):
    return 1.0 / (1.0 + jnp.exp2(x * (-LOG2E)))


def _gelu_tanh(x):
    c = float(np.sqrt(2.0 / np.pi))
    return 0.5 * x * (1.0 + jnp.tanh(c * (x + 0.044715 * (x * x * x))))


def _softplus(x):
    return jnp.maximum(x, 0.0) + jnp.log1p(jnp.exp(-jnp.abs(x)))


def _const_spec(shape):
    nd = len(shape)
    return pl.BlockSpec(shape, lambda *_: (0,) * nd, pipeline_mode=pl.Buffered(1))


def _stack_batches(ref, c, lo=0, width=None):
    width = c if width is None else width
    return jnp.concatenate([ref[:, b * c + lo:b * c + lo + width] for b in range(SUBLANES)], axis=0)


def _store_batches(ref, val, c, lo=0):
    steps = ref.shape[0]
    width = val.shape[1]
    for b in range(SUBLANES):
        ref[:, b * c + lo:b * c + lo + width] = val[b * steps:(b + 1) * steps].astype(ref.dtype)


def _perm_matrix():
    rows = SCAN_STEPS * SUBLANES
    p = np.zeros((rows, rows), np.float32)
    t, b = np.meshgrid(np.arange(SCAN_STEPS), np.arange(SUBLANES), indexing="ij")
    p[(t * SUBLANES + b).ravel(), (b * SCAN_STEPS + t).ravel()] = 1.0
    return p


def _halo_perm_matrices():
    prev_steps = CONV_W // 2
    halo_rows = SUBLANES * SUBLANES
    pp = np.zeros((prev_steps * SUBLANES, halo_rows), np.float32)
    pn = np.zeros((SUBLANES, halo_rows), np.float32)
    for b in range(SUBLANES):
        for j in range(prev_steps):
            pp[j * SUBLANES + b, b * SUBLANES + SUBLANES - prev_steps + j] = 1.0
        pn[b, b * SUBLANES] = 1.0
    return pp, pn


def _inproj_fwd_kernel(xprev_ref, x_ref, xnext_ref, g_ref, w_ref, b_ref, p_ref, pp_ref, pn_ref,
                       cw_ref, cb_ref, wa_ref, ba_ref, wi_ref, bi_ref, lam_ref,
                       q_ref, k_ref, v_ref, sga_ref, sgr_ref, u_ref, gy_ref, hf_ref, carry,
                       *, d_rec, d_model):
    t = pl.program_id(0)
    n_t = pl.num_programs(0)
    rows = SCAN_STEPS * SUBLANES
    prev_rows = (CONV_W // 2) * SUBLANES

    @pl.when(t == 0)
    def _():
        carry[...] = jnp.zeros_like(carry)

    def normed(ref):
        return _rms(ref[...].reshape(-1, d_model), g_ref[...]).astype(BF16)

    def permuted(p, hb):
        return jnp.dot(p[...], hb, preferred_element_type=F32).astype(BF16)

    def proj(lhs, lo, width):
        return jnp.dot(lhs, w_ref[:, lo:lo + width], preferred_element_type=F32) + b_ref[:, lo:lo + width]

    h = normed(x_ref)
    hp = permuted(p_ref, h)
    lhs_u = jnp.concatenate([permuted(pp_ref, normed(xprev_ref)), hp, permuted(pn_ref, normed(xnext_ref))], axis=0)
    lo_u = 3 * D_ATT
    lo_y = lo_u + d_rec
    lo_ga = lo_y + d_rec
    lo_gr = lo_ga + d_model

    def u_tile(kt):
        return proj(lhs_u, lo_u + kt * MXU_TILE, MXU_TILE)

    def rec_tile(kt, ucat):
        cs = slice(kt * MXU_TILE, (kt + 1) * MXU_TILE)
        u_ref[:, cs] = ucat[prev_rows:prev_rows + rows].astype(BF16)
        ucat = jnp.concatenate([jnp.where(t == 0, 0.0, ucat[:prev_rows]), ucat[prev_rows:prev_rows + rows],
                                jnp.where(t == n_t - 1, 0.0, ucat[prev_rows + rows:])], axis=0)
        h_fwd = _rglru_tile(ucat, kt, cw_ref, cb_ref, wa_ref, ba_ref, wi_ref, bi_ref, lam_ref, carry, reverse=False)
        hf_ref[:, cs] = h_fwd.astype(BF16)

    def att_tile(ref, lo, kt, scale):
        val = proj(h, lo + kt * PROJ_TILE, PROJ_TILE)
        _store_batches(ref, val if scale is None else val * scale, D_ATT, kt * PROJ_TILE)

    def gy_tile(kt):
        cs = slice(kt * PROJ_TILE, (kt + 1) * PROJ_TILE)
        gy_ref[:, cs] = _gelu_tanh(proj(hp, lo_y + kt * PROJ_TILE, PROJ_TILE)).astype(BF16)

    def gate_tile(ref, lo, kt):
        _store_batches(ref, _sigmoid(proj(h, lo + kt * PROJ_TILE, PROJ_TILE)), d_model, kt * PROJ_TILE)

    fillers = [functools.partial(att_tile, q_ref, 0, kt, HEAD_DIM ** -0.5) for kt in range(D_ATT // PROJ_TILE)]
    fillers += [functools.partial(att_tile, k_ref, D_ATT, kt, None) for kt in range(D_ATT // PROJ_TILE)]
    fillers += [functools.partial(att_tile, v_ref, 2 * D_ATT, kt, None) for kt in range(D_ATT // PROJ_TILE)]
    fillers += [functools.partial(gy_tile, kt) for kt in range(d_rec // PROJ_TILE)]
    fillers += [functools.partial(gate_tile, sga_ref, lo_ga, kt) for kt in range(d_model // PROJ_TILE)]
    fillers += [functools.partial(gate_tile, sgr_ref, lo_gr, kt) for kt in range(d_model // PROJ_TILE)]
    n_rec = d_rec // MXU_TILE
    per_rec = -(-len(fillers) // n_rec)
    ucat_next = u_tile(0)
    for kt in range(n_rec):
        ucat = ucat_next
        if kt + 1 < n_rec:
            ucat_next = u_tile(kt + 1)
        for f in fillers[kt * per_rec:(kt + 1) * per_rec]:
            f()
        rec_tile(kt, ucat)


def _inproj_fwd(x, g, w, b, perms, rec_params, d_rec):
    bsz, s, d = x.shape
    rows = SCAN_STEPS * SUBLANES
    halo_blocks = SCAN_STEPS // SUBLANES
    n_halo = s // SUBLANES

    def tm(width):
        return jax.ShapeDtypeStruct((s, bsz * width), BF16), pl.BlockSpec((SCAN_STEPS, bsz * width), lambda t: (t, 0))

    def sm(width):
        return jax.ShapeDtypeStruct((s * bsz, width), BF16), pl.BlockSpec((rows, width), lambda t: (t, 0))

    outs = [tm(D_ATT), tm(D_ATT), tm(D_ATT), tm(d), tm(d), sm(d_rec), sm(d_rec), sm(d_rec)]
    consts = [g, w, b, *perms, *rec_params]
    return pl.pallas_call(
        functools.partial(_inproj_fwd_kernel, d_rec=d_rec, d_model=d),
        grid=(s // SCAN_STEPS,),
        in_specs=[
            pl.BlockSpec((bsz, SUBLANES, d), lambda t: (0, jnp.maximum(t * halo_blocks - 1, 0), 0)),
            pl.BlockSpec((bsz, SCAN_STEPS, d), lambda t: (0, t, 0)),
            pl.BlockSpec((bsz, SUBLANES, d), lambda t: (0, jnp.minimum((t + 1) * halo_blocks, n_halo - 1), 0)),
        ] + [_const_spec(c.shape) for c in consts],
        out_specs=[o[1] for o in outs],
        out_shape=[o[0] for o in outs],
        scratch_shapes=[pltpu.VMEM((SUBLANES, d_rec), F32)],
        compiler_params=pltpu.CompilerParams(dimension_semantics=("arbitrary",), vmem_limit_bytes=VMEM_LIMIT),
        name="inproj_rglru_fwd",
    )(x, x, x, *consts)


def _attn_kernel(q_ref, k0_ref, k1_ref, k2_ref, v0_ref, v1_ref, v2_ref, tlo_ref, thi_ref, o_ref, bias_ref,
                 *, n_rows):
    band = pl.program_id(0)
    rows_q = ATT_BAND * GRID_W
    lane_row = lax.broadcasted_iota(jnp.int32, (GRID_W, LANES), 1) < HEAD_DIM

    @pl.when(pl.program_id(1) == 0)
    def _():
        r0 = band * ATT_BAND
        k0 = jnp.clip(r0 - WIN_H // 2, 0, n_rows - ATT_KEY_ROWS)
        for i in range(ATT_BAND):
            rq = r0 + i
            rs = jnp.clip(rq - WIN_H // 2, 0, n_rows - WIN_H)
            for jp in range(ATT_KEY_ROWS // HEADS_PER_VREG):
                rka = k0 + 2 * jp
                rkb = rka + 1
                va = jnp.logical_and(rka >= rs, rka < rs + WIN_H)
                vb = jnp.logical_and(rkb >= rs, rkb < rs + WIN_H)
                dra = jnp.clip(rka - rq + (WIN_H - 1), 0, 2 * WIN_H - 2)
                drb = jnp.clip(rkb - rq + (WIN_H - 1), 0, 2 * WIN_H - 2)
                for h in range(N_HEADS):
                    a = jnp.where(va, tlo_ref[h, dra], NEG)
                    b = jnp.where(vb, thi_ref[h, drb], NEG)
                    bias_ref[h, i * GRID_W:(i + 1) * GRID_W, jp * LANES:(jp + 1) * LANES] = jnp.where(lane_row, a, b)

    first = lax.broadcasted_iota(jnp.int32, (rows_q, LANES), 1) < HEAD_DIM
    k_refs = (k0_ref, k1_ref, k2_ref)
    v_refs = (v0_ref, v1_ref, v2_ref)
    zero = jnp.zeros((), BF16)

    def scores(hd):
        j, half = divmod(hd, HEADS_PER_VREG)
        cs = slice(j * LANES, (j + 1) * LANES)
        qh = jnp.where(first if half == 0 else jnp.logical_not(first), q_ref[:, cs], zero)
        s = jnp.concatenate(
            [lax.dot_general(qh, r[:, cs], (((1,), (1,)), ((), ())), preferred_element_type=F32) for r in k_refs],
            axis=1) + bias_ref[hd]
        return s, jnp.max(s, axis=-1, keepdims=True)

    nxt = scores(0)
    outs = []
    for hd in range(N_HEADS):
        j = hd // HEADS_PER_VREG
        cs = slice(j * LANES, (j + 1) * LANES)
        s, m = nxt
        if hd + 1 < N_HEADS:
            nxt = scores(hd + 1)
        p = jnp.exp(s - m)
        l = jnp.sum(p, axis=-1, keepdims=True)
        pb = p.astype(BF16)
        o = jnp.dot(pb[:, 0:MXU_TILE], v_refs[0][:, cs], preferred_element_type=F32)
        for jj in range(1, len(v_refs)):
            o = o + jnp.dot(pb[:, jj * MXU_TILE:(jj + 1) * MXU_TILE], v_refs[jj][:, cs], preferred_element_type=F32)
        outs.append(o / l)
        if len(outs) == HEADS_PER_VREG:
            o_ref[:, cs] = jnp.where(first, outs[0], outs[1]).astype(BF16)
            outs = []


def _attention(q, k, v, tlo, thi, bsz):
    s = q.shape[0]
    n_rows = s // GRID_W
    n_bands = n_rows // ATT_BAND
    rows_q = ATT_BAND * GRID_W
    n_kblk = ATT_KEY_ROWS * GRID_W // MXU_TILE
    assert rows_q == MXU_TILE and ATT_KEY_ROWS * GRID_W == n_kblk * MXU_TILE

    def kv_spec(jj):
        def idx(band, bi):
            return (jnp.clip(band - 1, 0, n_bands - n_kblk) + jj, bi)
        return pl.BlockSpec((MXU_TILE, D_ATT), idx)

    return pl.pallas_call(
        functools.partial(_attn_kernel, n_rows=n_rows),
        grid=(n_bands, bsz),
        in_specs=[pl.BlockSpec((rows_q, D_ATT), lambda band, bi: (band, bi))]
        + [kv_spec(jj) for jj in range(n_kblk)] * 2
        + [_const_spec(tlo.shape), _const_spec(thi.shape)],
        out_specs=pl.BlockSpec((rows_q, D_ATT), lambda band, bi: (band, bi)),
        out_shape=jax.ShapeDtypeStruct((s, bsz * D_ATT), BF16),
        scratch_shapes=[pltpu.VMEM((N_HEADS, rows_q, ATT_KEY_ROWS * GRID_W), F32)],
        compiler_params=pltpu.CompilerParams(
            dimension_semantics=("arbitrary", "arbitrary"), vmem_limit_bytes=VMEM_LIMIT),
        name="nbr_attention",
    )(q, k, k, k, v, v, v, tlo, thi)


def _bias_tables(rpb):
    qc = np.arange(GRID_W)[:, None]
    kc = np.arange(GRID_W)[None, :]
    ws = np.clip(qc - WIN_W // 2, 0, GRID_W - WIN_W)
    valid = (kc >= ws) & (kc < ws + WIN_W)
    d_col = np.clip(kc - qc, -(WIN_W - 1), WIN_W - 1) + (WIN_W - 1)
    onehot = (np.arange(2 * WIN_W - 1)[:, None, None] == d_col[None]).astype(np.float32)
    t = jnp.einsum("hrd,dqk->hrqk", rpb.astype(F32), onehot, precision=lax.Precision.HIGHEST)
    t = jnp.where(valid[None, None], t, NEG)
    neg = jnp.full_like(t, NEG)
    return jnp.concatenate([t, neg], axis=-1), jnp.concatenate([neg, t], axis=-1)


def _rglru_tile(ucat, kt, cw_ref, cb_ref, wa_ref, ba_ref, wi_ref, bi_ref, lam_ref, carry, reverse):
    rows = SCAN_STEPS * SUBLANES
    cs = slice(kt * MXU_TILE, (kt + 1) * MXU_TILE)
    conv = cb_ref[:, cs]
    for j in range(CONV_W):
        conv = conv + ucat[j * SUBLANES:j * SUBLANES + rows] * cw_ref[j:j + 1, cs]
    conv_b = conv.astype(BF16)

    def gate(w_ref, b_ref):
        return _sigmoid(jnp.dot(conv_b, w_ref[kt], preferred_element_type=F32) + b_ref[:, cs])

    r_gate = gate(wa_ref, ba_ref)
    i_gate = gate(wi_ref, bi_ref)
    log_a = r_gate * (-LRU_C * _softplus(-lam_ref[:, cs]))
    a = jnp.exp(log_a)
    z = jnp.maximum(jnp.tanh(log_a) * (-1.0 - a * a), 0.0)
    mult = jnp.where(z > 0.0, z * lax.rsqrt(z), 0.0)
    bx = mult * (i_gate * conv)

    h = carry[:, cs]
    hs = [None] * SCAN_STEPS
    for i in range(SCAN_STEPS):
        t = SCAN_STEPS - 1 - i if reverse else i
        sl = slice(t * SUBLANES, (t + 1) * SUBLANES)
        h = a[sl] * h + bx[sl]
        hs[t] = h
    carry[:, cs] = h
    return jnp.concatenate(hs, axis=0)


def _rglru_bwd_merge_kernel(uprev_ref, u_ref, unext_ref, cw_ref, cb_ref, wa_ref, ba_ref, wi_ref, bi_ref, lam_ref,
                            gy_ref, hf_ref, pt_ref, x_ref, att_ref, sga_ref, sgr_ref, wao_ref, wro_ref, wout_ref,
                            o_ref, carry, y_s):
    j = pl.program_id(0)
    n_t = pl.num_programs(0) - 1
    rows = SCAN_STEPS * SUBLANES
    d = x_ref.shape[-1]
    prev_rows = (CONV_W // 2) * SUBLANES

    @pl.when(j == 0)
    def _():
        carry[...] = jnp.zeros_like(carry)
        y_s[...] = jnp.zeros_like(y_s)

    t_idx = n_t - 1 - jnp.minimum(j, n_t - 1)
    y_nat = jnp.dot(pt_ref[...], y_s[...], preferred_element_type=F32).astype(BF16)
    att_nat = _stack_batches(att_ref, D_ATT)
    out = x_ref[...].reshape(rows, d)
    n_tiles = d // MXU_TILE

    def branch_tile(kt):
        cs = slice(kt * MXU_TILE, (kt + 1) * MXU_TILE)
        return (jnp.dot(y_nat, wro_ref[:, cs], preferred_element_type=F32),
                jnp.dot(att_nat, wao_ref[:, cs], preferred_element_type=F32))

    nxt = branch_tile(0)
    for kt in range(n_tiles):
        cs = slice(kt * MXU_TILE, (kt + 1) * MXU_TILE)
        y_rec, y_att = nxt
        if kt + 1 < n_tiles:
            nxt = branch_tile(kt + 1)
        mixed = (_stack_batches(sga_ref, d, kt * MXU_TILE, MXU_TILE).astype(F32) * y_att
                 + _stack_batches(sgr_ref, d, kt * MXU_TILE, MXU_TILE).astype(F32) * y_rec)
        out = out + jnp.dot(mixed.astype(BF16), wout_ref[cs, :], preferred_element_type=F32)

        prev = jnp.where(t_idx == 0, 0.0, uprev_ref[:, cs].astype(F32)[BF16_ROWS - prev_rows:])
        after = jnp.where(t_idx == n_t - 1, 0.0, unext_ref[:, cs].astype(F32)[0:SUBLANES])
        ucat = jnp.concatenate([prev, u_ref[:, cs].astype(F32), after], axis=0)
        h_bwd = _rglru_tile(ucat, kt, cw_ref, cb_ref, wa_ref, ba_ref, wi_ref, bi_ref, lam_ref, carry, reverse=True)
        y_s[:, cs] = ((hf_ref[:, cs].astype(F32) + h_bwd) * gy_ref[:, cs].astype(F32)).astype(BF16)
    o_ref[...] = out.reshape(o_ref.shape)


def _rglru_bwd_merge(u, params, gy, h_fwd, perm_t, x, att, sga, sgr, wao, wro, wout):
    n_rows, c = u.shape
    bsz, s, d = x.shape
    rows = SCAN_STEPS * SUBLANES
    n_t = s // SCAN_STEPS
    halo_per_block = rows // BF16_ROWS
    n_halo = n_rows // BF16_ROWS

    def scan_blk(j):
        return n_t - 1 - jnp.minimum(j, n_t - 1)

    def merge_blk(j):
        return n_t - 1 - jnp.maximum(j - 1, 0)

    sm_spec = pl.BlockSpec((rows, c), lambda j: (scan_blk(j), 0))
    halo_specs = [
        pl.BlockSpec((BF16_ROWS, c), lambda j: (jnp.maximum(scan_blk(j) * halo_per_block - 1, 0), 0)),
        sm_spec,
        pl.BlockSpec((BF16_ROWS, c), lambda j: (jnp.minimum((scan_blk(j) + 1) * halo_per_block, n_halo - 1), 0)),
    ]

    def tm(width):
        return pl.BlockSpec((SCAN_STEPS, bsz * width), lambda j: (merge_blk(j), 0))

    x_spec = pl.BlockSpec((bsz, SCAN_STEPS, d), lambda j: (0, merge_blk(j), 0))
    consts = [perm_t]
    weights = [wao, wro, wout]
    return pl.pallas_call(
        _rglru_bwd_merge_kernel,
        grid=(n_t + 1,),
        in_specs=halo_specs + [_const_spec(p.shape) for p in params] + [sm_spec, sm_spec]
        + [_const_spec(p.shape) for p in consts] + [x_spec, tm(D_ATT), tm(d), tm(d)]
        + [_const_spec(p.shape) for p in weights],
        out_specs=x_spec,
        out_shape=jax.ShapeDtypeStruct(x.shape, F32),
        scratch_shapes=[pltpu.VMEM((SUBLANES, c), F32), pltpu.VMEM((rows, c), BF16)],
        compiler_params=pltpu.CompilerParams(dimension_semantics=("arbitrary",), vmem_limit_bytes=VMEM_LIMIT),
        name="rglru_bwd_merge",
    )(u, u, u, *params, gy, h_fwd, *consts, x, att, sga, sgr, *weights)


def _gate_tiles(w):
    nb, bw, _ = w.shape
    per = MXU_TILE // bw
    eye = jnp.eye(per, dtype=w.dtype)
    t = jnp.einsum("cpij,pq->cpiqj", w.reshape(nb // per, per, bw, bw), eye)
    return t.reshape(nb // per, MXU_TILE, MXU_TILE).astype(BF16)


def _ffn_kernel(x_ref, g2_ref, w1_ref, w2_ref, gf_ref, o_ref, *, ff_chunk):
    x = x_ref[...]
    h = _rms(x, g2_ref[...]).astype(BF16)
    d_ff = w1_ref.shape[1]
    acc = x
    for c in range(d_ff // ff_chunk):
        cs = slice(c * ff_chunk, (c + 1) * ff_chunk)
        hid = jnp.maximum(jnp.dot(h, w1_ref[:, cs], preferred_element_type=F32), 0.0)
        acc = acc + jnp.dot((hid * hid).astype(BF16), w2_ref[cs, :], preferred_element_type=F32)
    o_ref[...] = _rms(acc, gf_ref[...])


def _ffn(x2d, g2, w1, w2, gf):
    n, d = x2d.shape
    spec = pl.BlockSpec((ROW_BLOCK, d), lambda i: (i, 0))
    return pl.pallas_call(
        functools.partial(_ffn_kernel, ff_chunk=1024),
        grid=(n // ROW_BLOCK,),
        in_specs=[spec, _const_spec(g2.shape), _const_spec(w1.shape), _const_spec(w2.shape), _const_spec(gf.shape)],
        out_specs=spec,
        out_shape=jax.ShapeDtypeStruct((n, d), F32),
        compiler_params=pltpu.CompilerParams(dimension_semantics=("parallel",), vmem_limit_bytes=VMEM_LIMIT),
        name="ffn_final_norm",
    )(x2d, g2, w1, w2, gf)


def kernel(x, ln1_g, w_in, b_in, rpb, w_att_o, conv_w, conv_b, w_rg_a, b_rg_a, w_rg_i, b_rg_i, lru_lambda,
           w_rec_o, w_out, ln2_g, w_ff1, w_ff2, lnf_g):
    bsz, s, d = x.shape
    d_rec = conv_w.shape[-1]
    assert ln1_g.shape[0] == 1, "single-layer stack only"
    assert bsz == SUBLANES and s % SCAN_STEPS == 0 and s % (ATT_BAND * GRID_W) == 0 and (bsz * s) % ROW_BLOCK == 0
    assert w_in.shape[-1] == 3 * D_ATT + 2 * d_rec + 2 * d and d_rec % MXU_TILE == 0 and d_rec == d
    row = lambda p: p.reshape(1, -1).astype(F32)
    l = 0

    def dir_params(di):
        return [conv_w[l].astype(F32), row(conv_b[l]), _gate_tiles(w_rg_a[l, di]), row(b_rg_a[l, di]),
                _gate_tiles(w_rg_i[l, di]), row(b_rg_i[l, di]), row(lru_lambda[l, di])]

    perm = _perm_matrix()
    perms = [jnp.asarray(m, BF16) for m in (perm, *_halo_perm_matrices())]
    q, k, v, sga, sgr, u, gy, h_fwd = _inproj_fwd(x, row(ln1_g[l]), w_in[l].astype(BF16), row(b_in[l]), perms,
                                                  dir_params(0), d_rec)

    tlo, thi = _bias_tables(rpb[l])
    att = _attention(q, k, v, tlo, thi, bsz)

    x1 = _rglru_bwd_merge(u, dir_params(1), gy, h_fwd, jnp.asarray(perm.T, BF16), x, att, sga, sgr,
                          w_att_o[l].astype(BF16), w_rec_o[l].astype(BF16), w_out[l].astype(BF16))

    out = _ffn(x1.reshape(bsz * s, d), row(ln2_g[l]), w_ff1[l].astype(BF16), w_ff2[l].astype(BF16), row(lnf_g))
    return out.reshape(bsz, s, d)
```

```python
import functools

import jax
import jax.numpy as jnp
import numpy as np
from jax import lax
from jax.experimental import pallas as pl
from jax.experimental.pallas import tpu as pltpu

F32 = jnp.float32
BF16 = jnp.bfloat16

GRID_W = 64
N_HEADS = 8
HEAD_DIM = 64
D_ATT = N_HEADS * HEAD_DIM
WIN_H = 8
WIN_W = 16
CONV_W = 4
LRU_C = 8.0
EPS = 1e-6
NEG = -1e30
LOG2E = float(np.log2(np.e))

LANES = 128
SUBLANES = 8
BF16_ROWS = 16
MXU_TILE = 256
VMEM_LIMIT = 56 * 1024 * 1024
VMEM_LIMIT_TAIL = 60 * 1024 * 1024

PROJ_TILE = 256
FF_CHUNK = 1024
SCAN_STEPS = 64
ATT_BAND = 4
ATT_KEY_ROWS = 12
HEADS_PER_VREG = LANES // HEAD_DIM


def _rms(x, g):
    ms = jnp.mean(x * x, axis=-1, keepdims=True)
    return x * lax.rsqrt(ms + EPS) * g


def _sigmoid(x):
    return 1.0 / (1.0 + jnp.exp2(x * (-LOG2E)))


def _gelu_tanh(x):
    c = float(np.sqrt(2.0 / np.pi))
    return 0.5 * x * (1.0 + jnp.tanh(c * (x + 0.044715 * (x * x * x))))


def _softplus(x):
    return jnp.maximum(x, 0.0) + jnp.log1p(jnp.exp(-jnp.abs(x)))


def _const_spec(shape):
    nd = len(shape)
    return pl.BlockSpec(shape, lambda *_: (0,) * nd, pipeline_mode=pl.Buffered(1))


def _stack_batches(ref, c, lo=0, width=None):
    width = c if width is None else width
    return jnp.concatenate([ref[:, b * c + lo:b * c + lo + width] for b in range(SUBLANES)], axis=0)


def _store_batches(ref, val, c, lo=0):
    steps = ref.shape[0]
    width = val.shape[1]
    for b in range(SUBLANES):
        ref[:, b * c + lo:b * c + lo + width] = val[b * steps:(b + 1) * steps].astype(ref.dtype)


def _perm_matrix():
    rows = SCAN_STEPS * SUBLANES
    p = np.zeros((rows, rows), np.float32)
    t, b = np.meshgrid(np.arange(SCAN_STEPS), np.arange(SUBLANES), indexing="ij")
    p[(t * SUBLANES + b).ravel(), (b * SCAN_STEPS + t).ravel()] = 1.0
    return p


def _halo_perm_matrices():
    prev_steps = CONV_W // 2
    halo_rows = SUBLANES * SUBLANES
    pp = np.zeros((prev_steps * SUBLANES, halo_rows), np.float32)
    pn = np.zeros((SUBLANES, halo_rows), np.float32)
    for b in range(SUBLANES):
        for j in range(prev_steps):
            pp[j * SUBLANES + b, b * SUBLANES + SUBLANES - prev_steps + j] = 1.0
        pn[b, b * SUBLANES] = 1.0
    return pp, pn


def _inproj_fwd_kernel(xprev_ref, x_ref, xnext_ref, g_ref, w_ref, b_ref, p_ref, pp_ref, pn_ref,
                       cw_ref, cb_ref, wa_ref, ba_ref, wi_ref, bi_ref, lam_ref,
                       q_ref, k_ref, v_ref, sga_ref, sgr_ref, u_ref, gy_ref, hf_ref, carry,
                       *, d_rec, d_model):
    t = pl.program_id(0)
    n_t = pl.num_programs(0)
    rows = SCAN_STEPS * SUBLANES
    prev_rows = (CONV_W // 2) * SUBLANES

    @pl.when(t == 0)
    def _():
        carry[...] = jnp.zeros_like(carry)

    def normed(ref):
        return _rms(ref[...].reshape(-1, d_model), g_ref[...]).astype(BF16)

    def permuted(p, hb):
        return jnp.dot(p[...], hb, preferred_element_type=F32).astype(BF16)

    def proj(lhs, lo, width):
        return jnp.dot(lhs, w_ref[:, lo:lo + width], preferred_element_type=F32) + b_ref[:, lo:lo + width]

    h = normed(x_ref)
    hp = permuted(p_ref, h)
    lhs_u = jnp.concatenate([permuted(pp_ref, normed(xprev_ref)), hp, permuted(pn_ref, normed(xnext_ref))], axis=0)
    lo_u = 3 * D_ATT
    lo_y = lo_u + d_rec
    lo_ga = lo_y + d_rec
    lo_gr = lo_ga + d_model

    def u_tile(kt):
        return proj(lhs_u, lo_u + kt * MXU_TILE, MXU_TILE)

    def rec_tile(kt, ucat):
        cs = slice(kt * MXU_TILE, (kt + 1) * MXU_TILE)
        u_ref[:, cs] = ucat[prev_rows:prev_rows + rows].astype(BF16)
        ucat = jnp.concatenate([jnp.where(t == 0, 0.0, ucat[:prev_rows]), ucat[prev_rows:prev_rows + rows],
                                jnp.where(t == n_t - 1, 0.0, ucat[prev_rows + rows:])], axis=0)
        h_fwd = _rglru_tile(ucat, kt, cw_ref, cb_ref, wa_ref, ba_ref, wi_ref, bi_ref, lam_ref, carry, reverse=False)
        hf_ref[:, cs] = h_fwd.astype(BF16)

    def att_tile(ref, lo, kt, scale):
        val = proj(h, lo + kt * PROJ_TILE, PROJ_TILE)
        _store_batches(ref, val if scale is None else val * scale, D_ATT, kt * PROJ_TILE)

    def gy_tile(kt):
        cs = slice(kt * PROJ_TILE, (kt + 1) * PROJ_TILE)
        gy_ref[:, cs] = _gelu_tanh(proj(hp, lo_y + kt * PROJ_TILE, PROJ_TILE)).astype(BF16)

    def gate_tile(ref, lo, kt):
        _store_batches(ref, _sigmoid(proj(h, lo + kt * PROJ_TILE, PROJ_TILE)), d_model, kt * PROJ_TILE)

    fillers = [functools.partial(att_tile, q_ref, 0, kt, HEAD_DIM ** -0.5) for kt in range(D_ATT // PROJ_TILE)]
    fillers += [functools.partial(att_tile, k_ref, D_ATT, kt, None) for kt in range(D_ATT // PROJ_TILE)]
    fillers += [functools.partial(att_tile, v_ref, 2 * D_ATT, kt, None) for kt in range(D_ATT // PROJ_TILE)]
    fillers += [functools.partial(gy_tile, kt) for kt in range(d_rec // PROJ_TILE)]
    fillers += [functools.partial(gate_tile, sga_ref, lo_ga, kt) for kt in range(d_model // PROJ_TILE)]
    fillers += [functools.partial(gate_tile, sgr_ref, lo_gr, kt) for kt in range(d_model // PROJ_TILE)]
    n_rec = d_rec // MXU_TILE
    per_rec = -(-len(fillers) // n_rec)
    ucat_next = u_tile(0)
    for kt in range(n_rec):
        ucat = ucat_next
        if kt + 1 < n_rec:
            ucat_next = u_tile(kt + 1)
        for f in fillers[kt * per_rec:(kt + 1) * per_rec]:
            f()
        rec_tile(kt, ucat)


def _inproj_fwd(x, g, w, b, perms, rec_params, d_rec):
    bsz, s, d = x.shape
    rows = SCAN_STEPS * SUBLANES
    halo_blocks = SCAN_STEPS // SUBLANES
    n_halo = s // SUBLANES

    def tm(width):
        return jax.ShapeDtypeStruct((s, bsz * width), BF16), pl.BlockSpec((SCAN_STEPS, bsz * width), lambda t: (t, 0))

    def sm(width):
        return jax.ShapeDtypeStruct((s * bsz, width), BF16), pl.BlockSpec((rows, width), lambda t: (t, 0))

    outs = [tm(D_ATT), tm(D_ATT), tm(D_ATT), tm(d), tm(d), sm(d_rec), sm(d_rec), sm(d_rec)]
    consts = [g, w, b, *perms, *rec_params]
    return pl.pallas_call(
        functools.partial(_inproj_fwd_kernel, d_rec=d_rec, d_model=d),
        grid=(s // SCAN_STEPS,),
        in_specs=[
            pl.BlockSpec((bsz, SUBLANES, d), lambda t: (0, jnp.maximum(t * halo_blocks - 1, 0), 0)),
            pl.BlockSpec((bsz, SCAN_STEPS, d), lambda t: (0, t, 0)),
            pl.BlockSpec((bsz, SUBLANES, d), lambda t: (0, jnp.minimum((t + 1) * halo_blocks, n_halo - 1), 0)),
        ] + [_const_spec(c.shape) for c in consts],
        out_specs=[o[1] for o in outs],
        out_shape=[o[0] for o in outs],
        scratch_shapes=[pltpu.VMEM((SUBLANES, d_rec), F32)],
        compiler_params=pltpu.CompilerParams(dimension_semantics=("arbitrary",), vmem_limit_bytes=VMEM_LIMIT),
        name="inproj_rglru_fwd",
    )(x, x, x, *consts)


def _attn_kernel(q_ref, k0_ref, k1_ref, k2_ref, v0_ref, v1_ref, v2_ref, tlo_ref, thi_ref, o_ref, bias_ref,
                 *, n_rows):
    band = pl.program_id(0)
    rows_q = ATT_BAND * GRID_W
    lane_row = lax.broadcasted_iota(jnp.int32, (GRID_W, LANES), 1) < HEAD_DIM

    @pl.when(pl.program_id(1) == 0)
    def _():
        r0 = band * ATT_BAND
        k0 = jnp.clip(r0 - WIN_H // 2, 0, n_rows - ATT_KEY_ROWS)
        for i in range(ATT_BAND):
            rq = r0 + i
            rs = jnp.clip(rq - WIN_H // 2, 0, n_rows - WIN_H)
            for jp in range(ATT_KEY_ROWS // HEADS_PER_VREG):
                rka = k0 + 2 * jp
                rkb = rka + 1
                va = jnp.logical_and(rka >= rs, rka < rs + WIN_H)
                vb = jnp.logical_and(rkb >= rs, rkb < rs + WIN_H)
                dra = jnp.clip(rka - rq + (WIN_H - 1), 0, 2 * WIN_H - 2)
                drb = jnp.clip(rkb - rq + (WIN_H - 1), 0, 2 * WIN_H - 2)
                for h in range(N_HEADS):
                    a = jnp.where(va, tlo_ref[h, dra], NEG)
                    b = jnp.where(vb, thi_ref[h, drb], NEG)
                    bias_ref[h, i * GRID_W:(i + 1) * GRID_W, jp * LANES:(jp + 1) * LANES] = jnp.where(lane_row, a, b)

    first = lax.broadcasted_iota(jnp.int32, (rows_q, LANES), 1) < HEAD_DIM
    k_refs = (k0_ref, k1_ref, k2_ref)
    v_refs = (v0_ref, v1_ref, v2_ref)
    zero = jnp.zeros((), BF16)

    def scores(hd):
        j, half = divmod(hd, HEADS_PER_VREG)
        cs = slice(j * LANES, (j + 1) * LANES)
        qh = jnp.where(first if half == 0 else jnp.logical_not(first), q_ref[:, cs], zero)
        s = jnp.concatenate(
            [lax.dot_general(qh, r[:, cs], (((1,), (1,)), ((), ())), preferred_element_type=F32) for r in k_refs],
            axis=1) + bias_ref[hd]
        return s, jnp.max(s, axis=-1, keepdims=True)

    nxt = scores(0)
    outs = []
    for hd in range(N_HEADS):
        j = hd // HEADS_PER_VREG
        cs = slice(j * LANES, (j + 1) * LANES)
        s, m = nxt
        if hd + 1 < N_HEADS:
            nxt = scores(hd + 1)
        p = jnp.exp(s - m)
        l = jnp.sum(p, axis=-1, keepdims=True)
        pb = p.astype(BF16)
        o = jnp.dot(pb[:, 0:MXU_TILE], v_refs[0][:, cs], preferred_element_type=F32)
        for jj in range(1, len(v_refs)):
            o = o + jnp.dot(pb[:, jj * MXU_TILE:(jj + 1) * MXU_TILE], v_refs[jj][:, cs], preferred_element_type=F32)
        outs.append(o / l)
        if len(outs) == HEADS_PER_VREG:
            o_ref[:, cs] = jnp.where(first, outs[0], outs[1]).astype(BF16)
            outs = []


def _attention(q, k, v, tlo, thi, bsz):
    s = q.shape[0]
    n_rows = s // GRID_W
    n_bands = n_rows // ATT_BAND
    rows_q = ATT_BAND * GRID_W
    n_kblk = ATT_KEY_ROWS * GRID_W // MXU_TILE
    assert rows_q == MXU_TILE and ATT_KEY_ROWS * GRID_W == n_kblk * MXU_TILE

    def kv_spec(jj):
        def idx(band, bi):
            return (jnp.clip(band - 1, 0, n_bands - n_kblk) + jj, bi)
        return pl.BlockSpec((MXU_TILE, D_ATT), idx)

    return pl.pallas_call(
        functools.partial(_attn_kernel, n_rows=n_rows),
        grid=(n_bands, bsz),
        in_specs=[pl.BlockSpec((rows_q, D_ATT), lambda band, bi: (band, bi))]
        + [kv_spec(jj) for jj in range(n_kblk)] * 2
        + [_const_spec(tlo.shape), _const_spec(thi.shape)],
        out_specs=pl.BlockSpec((rows_q, D_ATT), lambda band, bi: (band, bi)),
        out_shape=jax.ShapeDtypeStruct((s, bsz * D_ATT), BF16),
        scratch_shapes=[pltpu.VMEM((N_HEADS, rows_q, ATT_KEY_ROWS * GRID_W), F32)],
        compiler_params=pltpu.CompilerParams(
            dimension_semantics=("arbitrary", "arbitrary"), vmem_limit_bytes=VMEM_LIMIT),
        name="nbr_attention",
    )(q, k, k, k, v, v, v, tlo, thi)


def _bias_tables(rpb):
    qc = np.arange(GRID_W)[:, None]
    kc = np.arange(GRID_W)[None, :]
    ws = np.clip(qc - WIN_W // 2, 0, GRID_W - WIN_W)
    valid = (kc >= ws) & (kc < ws + WIN_W)
    d_col = np.clip(kc - qc, -(WIN_W - 1), WIN_W - 1) + (WIN_W - 1)
    onehot = (np.arange(2 * WIN_W - 1)[:, None, None] == d_col[None]).astype(np.float32)
    t = jnp.einsum("hrd,dqk->hrqk", rpb.astype(F32), onehot, precision=lax.Precision.HIGHEST)
    t = jnp.where(valid[None, None], t, NEG)
    neg = jnp.full_like(t, NEG)
    return jnp.concatenate([t, neg], axis=-1), jnp.concatenate([neg, t], axis=-1)


def _rglru_tile(ucat, kt, cw_ref, cb_ref, wa_ref, ba_ref, wi_ref, bi_ref, lam_ref, carry, reverse):
    rows = SCAN_STEPS * SUBLANES
    cs = slice(kt * MXU_TILE, (kt + 1) * MXU_TILE)
    conv = cb_ref[:, cs]
    for j in range(CONV_W):
        conv = conv + ucat[j * SUBLANES:j * SUBLANES + rows] * cw_ref[j:j + 1, cs]
    conv_b = conv.astype(BF16)

    def gate(w_ref, b_ref):
        return _sigmoid(jnp.dot(conv_b, w_ref[kt], preferred_element_type=F32) + b_ref[:, cs])

    r_gate = gate(wa_ref, ba_ref)
    i_gate = gate(wi_ref, bi_ref)
    log_a = r_gate * (-LRU_C * _softplus(-lam_ref[:, cs]))
    a = jnp.exp(log_a)
    z = jnp.maximum(jnp.tanh(log_a) * (-1.0 - a * a), 0.0)
    mult = jnp.where(z > 0.0, z * lax.rsqrt(z), 0.0)
    bx = mult * (i_gate * conv)

    h = carry[:, cs]
    hs = [None] * SCAN_STEPS
    for i in range(SCAN_STEPS):
        t = SCAN_STEPS - 1 - i if reverse else i
        sl = slice(t * SUBLANES, (t + 1) * SUBLANES)
        h = a[sl] * h + bx[sl]
        hs[t] = h
    carry[:, cs] = h
    return jnp.concatenate(hs, axis=0)


def _tail_kernel(uprev_ref, u_ref, unext_ref, cw_ref, cb_ref, wa_ref, ba_ref, wi_ref, bi_ref, lam_ref,
                 gy_ref, hf_ref, pt_ref, x_ref, att_ref, sga_ref, sgr_ref, wao_ref, wro_ref, wout_ref,
                 g2_ref, w1_ref, w2_ref, gf_ref, o_ref, carry, y_s, *, ff_chunk):
    j = pl.program_id(0)
    n_t = pl.num_programs(0) - 1
    rows = SCAN_STEPS * SUBLANES
    d = x_ref.shape[-1]
    prev_rows = (CONV_W // 2) * SUBLANES

    @pl.when(j == 0)
    def _():
        carry[...] = jnp.zeros_like(carry)
        y_s[...] = jnp.zeros_like(y_s)

    t_idx = n_t - 1 - jnp.minimum(j, n_t - 1)
    y_nat = jnp.dot(pt_ref[...], y_s[...], preferred_element_type=F32).astype(BF16)
    n_tiles = d // MXU_TILE

    def rec_tile(kt):
        cs = slice(kt * MXU_TILE, (kt + 1) * MXU_TILE)
        prev = jnp.where(t_idx == 0, 0.0, uprev_ref[:, cs].astype(F32)[BF16_ROWS - prev_rows:])
        after = jnp.where(t_idx == n_t - 1, 0.0, unext_ref[:, cs].astype(F32)[0:SUBLANES])
        ucat = jnp.concatenate([prev, u_ref[:, cs].astype(F32), after], axis=0)
        h_bwd = _rglru_tile(ucat, kt, cw_ref, cb_ref, wa_ref, ba_ref, wi_ref, bi_ref, lam_ref, carry, reverse=True)
        y_s[:, cs] = ((hf_ref[:, cs].astype(F32) + h_bwd) * gy_ref[:, cs].astype(F32)).astype(BF16)

    y_rec = jnp.dot(y_nat, wro_ref[...], preferred_element_type=F32)
    y_att = jnp.dot(_stack_batches(att_ref, D_ATT), wao_ref[...], preferred_element_type=F32)
    mixed = _stack_batches(sga_ref, d).astype(F32) * y_att + _stack_batches(sgr_ref, d).astype(F32) * y_rec
    out = x_ref[...].reshape(rows, d) + jnp.dot(mixed.astype(BF16), wout_ref[...], preferred_element_type=F32)

    h2 = _rms(out, g2_ref[...]).astype(BF16)
    d_ff = w1_ref.shape[1]
    n_chunks = d_ff // ff_chunk
    for c in range(max(n_chunks, n_tiles)):
        if c < n_chunks:
            fs = slice(c * ff_chunk, (c + 1) * ff_chunk)
            hid = jnp.maximum(jnp.dot(h2, w1_ref[:, fs], preferred_element_type=F32), 0.0)
            out = out + jnp.dot((hid * hid).astype(BF16), w2_ref[fs, :], preferred_element_type=F32)
        if c < n_tiles:
            rec_tile(c)
    o_ref[...] = _rms(out, gf_ref[...]).reshape(o_ref.shape)


def _tail(u, params, gy, h_fwd, perm_t, x, att, sga, sgr, wao, wro, wout, g2, w1, w2, gf):
    n_rows, c = u.shape
    bsz, s, d = x.shape
    rows = SCAN_STEPS * SUBLANES
    n_t = s // SCAN_STEPS
    halo_per_block = rows // BF16_ROWS
    n_halo = n_rows // BF16_ROWS

    def scan_blk(j):
        return n_t - 1 - jnp.minimum(j, n_t - 1)

    def merge_blk(j):
        return n_t - 1 - jnp.maximum(j - 1, 0)

    sm_spec = pl.BlockSpec((rows, c), lambda j: (scan_blk(j), 0))
    halo_specs = [
        pl.BlockSpec((BF16_ROWS, c), lambda j: (jnp.maximum(scan_blk(j) * halo_per_block - 1, 0), 0)),
        sm_spec,
        pl.BlockSpec((BF16_ROWS, c), lambda j: (jnp.minimum((scan_blk(j) + 1) * halo_per_block, n_halo - 1), 0)),
    ]

    def tm(width):
        return pl.BlockSpec((SCAN_STEPS, bsz * width), lambda j: (merge_blk(j), 0))

    x_spec = pl.BlockSpec((bsz, SCAN_STEPS, d), lambda j: (0, merge_blk(j), 0))
    consts = [perm_t]
    weights = [wao, wro, wout, g2, w1, w2, gf]
    return pl.pallas_call(
        functools.partial(_tail_kernel, ff_chunk=FF_CHUNK),
        grid=(n_t + 1,),
        in_specs=halo_specs + [_const_spec(p.shape) for p in params] + [sm_spec, sm_spec]
        + [_const_spec(p.shape) for p in consts] + [x_spec, tm(D_ATT), tm(d), tm(d)]
        + [_const_spec(p.shape) for p in weights],
        out_specs=x_spec,
        out_shape=jax.ShapeDtypeStruct(x.shape, F32),
        scratch_shapes=[pltpu.VMEM((SUBLANES, c), F32), pltpu.VMEM((rows, c), BF16)],
        compiler_params=pltpu.CompilerParams(dimension_semantics=("arbitrary",), vmem_limit_bytes=VMEM_LIMIT_TAIL),
        name="rglru_bwd_merge_ffn",
    )(u, u, u, *params, gy, h_fwd, *consts, x, att, sga, sgr, *weights)


def _gate_tiles(w):
    nb, bw, _ = w.shape
    per = MXU_TILE // bw
    eye = jnp.eye(per, dtype=w.dtype)
    t = jnp.einsum("cpij,pq->cpiqj", w.reshape(nb // per, per, bw, bw), eye)
    return t.reshape(nb // per, MXU_TILE, MXU_TILE).astype(BF16)


def kernel(x, ln1_g, w_in, b_in, rpb, w_att_o, conv_w, conv_b, w_rg_a, b_rg_a, w_rg_i, b_rg_i, lru_lambda,
           w_rec_o, w_out, ln2_g, w_ff1, w_ff2, lnf_g):
    bsz, s, d = x.shape
    d_rec = conv_w.shape[-1]
    assert ln1_g.shape[0] == 1, "single-layer stack only"
    assert bsz == SUBLANES and s % SCAN_STEPS == 0 and s % (ATT_BAND * GRID_W) == 0 and w_ff1.shape[-1] % FF_CHUNK == 0
    assert w_in.shape[-1] == 3 * D_ATT + 2 * d_rec + 2 * d and d_rec % MXU_TILE == 0 and d_rec == d
    row = lambda p: p.reshape(1, -1).astype(F32)
    l = 0

    def dir_params(di):
        return [conv_w[l].astype(F32), row(conv_b[l]), _gate_tiles(w_rg_a[l, di]), row(b_rg_a[l, di]),
                _gate_tiles(w_rg_i[l, di]), row(b_rg_i[l, di]), row(lru_lambda[l, di])]

    perm = _perm_matrix()
    perms = [jnp.asarray(m, BF16) for m in (perm, *_halo_perm_matrices())]
    q, k, v, sga, sgr, u, gy, h_fwd = _inproj_fwd(x, row(ln1_g[l]), w_in[l].astype(BF16), row(b_in[l]), perms,
                                                  dir_params(0), d_rec)

    tlo, thi = _bias_tables(rpb[l])
    att = _attention(q, k, v, tlo, thi, bsz)

    return _tail(u, dir_params(1), gy, h_fwd, jnp.asarray(perm.T, BF16), x, att, sga, sgr,
                 w_att_o[l].astype(BF16), w_rec_o[l].astype(BF16), w_out[l].astype(BF16),
                 row(ln2_g[l]), w_ff1[l].astype(BF16), w_ff2[l].astype(BF16), row(lnf_g))
```

```python
import functools

import jax
import jax.numpy as jnp
import numpy as np
from jax import lax
from jax.experimental import pallas as pl
from jax.experimental.pallas import tpu as pltpu

F32 = jnp.float32
BF16 = jnp.bfloat16

GRID_W = 64
N_HEADS = 8
HEAD_DIM = 64
D_ATT = N_HEADS * HEAD_DIM
WIN_H = 8
WIN_W = 16
CONV_W = 4
LRU_C = 8.0
EPS = 1e-6
NEG = -1e30
LOG2E = float(np.log2(np.e))

LANES = 128
SUBLANES = 8
BF16_ROWS = 16
MXU_TILE = 256
VMEM_LIMIT = 56 * 1024 * 1024
VMEM_LIMIT_TAIL = 60 * 1024 * 1024

PROJ_TILE = 256
FF_CHUNK = 1024
SCAN_STEPS = 64
PERM_PITCH = SCAN_STEPS + SUBLANES
ATT_BAND = 4
ATT_KEY_ROWS = 12
HEADS_PER_VREG = LANES // HEAD_DIM


def _rms(x, g):
    ms = jnp.mean(x * x, axis=-1, keepdims=True)
    return x * lax.rsqrt(ms + EPS) * g


def _sigmoid(x):
    return 1.0 / (1.0 + jnp.exp2(x * (-LOG2E)))


def _gelu_tanh(x):
    c = float(np.sqrt(2.0 / np.pi))
    return 0.5 * x * (1.0 + jnp.tanh(c * (x + 0.044715 * (x * x * x))))


def _softplus(x):
    return jnp.maximum(x, 0.0) + jnp.log1p(jnp.exp(-jnp.abs(x)))


def _const_spec(shape):
    nd = len(shape)
    return pl.BlockSpec(shape, lambda *_: (0,) * nd, pipeline_mode=pl.Buffered(1))


def _stack_batches(ref, c, lo=0, width=None):
    width = c if width is None else width
    return jnp.concatenate([ref[:, b * c + lo:b * c + lo + width] for b in range(SUBLANES)], axis=0)


def _store_batches(ref, val, c, lo=0):
    steps = ref.shape[0]
    width = val.shape[1]
    for b in range(SUBLANES):
        ref[:, b * c + lo:b * c + lo + width] = val[b * steps:(b + 1) * steps].astype(ref.dtype)


def _halo_perm_matrices():
    prev_steps = CONV_W // 2
    halo_rows = SUBLANES * SUBLANES
    pp = np.zeros((prev_steps * SUBLANES, halo_rows), np.float32)
    pn = np.zeros((SUBLANES, halo_rows), np.float32)
    for b in range(SUBLANES):
        for j in range(prev_steps):
            pp[j * SUBLANES + b, b * SUBLANES + SUBLANES - prev_steps + j] = 1.0
        pn[b, b * SUBLANES] = 1.0
    return pp, pn


def _inproj_fwd_kernel(xprev_ref, x_ref, xnext_ref, g_ref, w_ref, b_ref, pp_ref, pn_ref,
                       cw_ref, cb_ref, wa_ref, ba_ref, wi_ref, bi_ref, lam_ref,
                       q_ref, k_ref, v_ref, sga_ref, sgr_ref, u_ref, gy_ref, hf_ref, carry, perm_s,
                       *, d_rec, d_model):
    t = pl.program_id(0)
    n_t = pl.num_programs(0)
    rows = SCAN_STEPS * SUBLANES
    prev_rows = (CONV_W // 2) * SUBLANES

    @pl.when(t == 0)
    def _():
        carry[...] = jnp.zeros_like(carry)

    def normed(ref):
        return _rms(ref[...].reshape(-1, d_model), g_ref[...])

    def permuted(p, hb):
        return jnp.dot(p[...], hb, preferred_element_type=F32).astype(BF16)

    def proj(lhs, lo, width):
        return jnp.dot(lhs, w_ref[:, lo:lo + width], preferred_element_type=F32) + b_ref[:, lo:lo + width]

    def step_major(hb):
        for c in range(d_model // LANES):
            for b in range(SUBLANES):
                perm_s[c, b * PERM_PITCH:b * PERM_PITCH + SCAN_STEPS, :] = (
                    hb[b * SCAN_STEPS:(b + 1) * SCAN_STEPS, c * LANES:(c + 1) * LANES])
        return jnp.concatenate(
            [jnp.concatenate([perm_s[c, pl.ds(ts, SUBLANES, stride=PERM_PITCH), :] for c in range(d_model // LANES)],
                             axis=1) for ts in range(SCAN_STEPS)], axis=0)

    h32 = normed(x_ref)
    h = h32.astype(BF16)
    hp = step_major(h32).astype(BF16)
    lhs_u = jnp.concatenate([permuted(pp_ref, normed(xprev_ref).astype(BF16)), hp,
                             permuted(pn_ref, normed(xnext_ref).astype(BF16))], axis=0)
    lo_u = 3 * D_ATT
    lo_y = lo_u + d_rec
    lo_ga = lo_y + d_rec
    lo_gr = lo_ga + d_model

    def u_tile(kt):
        return proj(lhs_u, lo_u + kt * MXU_TILE, MXU_TILE)

    def rec_tile(kt, ucat):
        cs = slice(kt * MXU_TILE, (kt + 1) * MXU_TILE)
        u_ref[:, cs] = ucat[prev_rows:prev_rows + rows].astype(BF16)
        ucat = jnp.concatenate([jnp.where(t == 0, 0.0, ucat[:prev_rows]), ucat[prev_rows:prev_rows + rows],
                                jnp.where(t == n_t - 1, 0.0, ucat[prev_rows + rows:])], axis=0)
        h_fwd = _rglru_tile(ucat, kt, cw_ref, cb_ref, wa_ref, ba_ref, wi_ref, bi_ref, lam_ref, carry, reverse=False)
        hf_ref[:, cs] = h_fwd.astype(BF16)

    def att_tile(ref, lo, kt, scale):
        val = proj(h, lo + kt * PROJ_TILE, PROJ_TILE)
        _store_batches(ref, val if scale is None else val * scale, D_ATT, kt * PROJ_TILE)

    def gy_tile(kt):
        cs = slice(kt * PROJ_TILE, (kt + 1) * PROJ_TILE)
        gy_ref[:, cs] = _gelu_tanh(proj(hp, lo_y + kt * PROJ_TILE, PROJ_TILE)).astype(BF16)

    def gate_tile(ref, lo, kt):
        _store_batches(ref, _sigmoid(proj(h, lo + kt * PROJ_TILE, PROJ_TILE)), d_model, kt * PROJ_TILE)

    fillers = [functools.partial(att_tile, q_ref, 0, kt, HEAD_DIM ** -0.5 * LOG2E) for kt in range(D_ATT // PROJ_TILE)]
    fillers += [functools.partial(att_tile, k_ref, D_ATT, kt, None) for kt in range(D_ATT // PROJ_TILE)]
    fillers += [functools.partial(att_tile, v_ref, 2 * D_ATT, kt, None) for kt in range(D_ATT // PROJ_TILE)]
    fillers += [functools.partial(gy_tile, kt) for kt in range(d_rec // PROJ_TILE)]
    fillers += [functools.partial(gate_tile, sga_ref, lo_ga, kt) for kt in range(d_model // PROJ_TILE)]
    fillers += [functools.partial(gate_tile, sgr_ref, lo_gr, kt) for kt in range(d_model // PROJ_TILE)]
    n_rec = d_rec // MXU_TILE
    per_rec = -(-len(fillers) // n_rec)
    ucat_next = u_tile(0)
    for kt in range(n_rec):
        ucat = ucat_next
        if kt + 1 < n_rec:
            ucat_next = u_tile(kt + 1)
        for f in fillers[kt * per_rec:(kt + 1) * per_rec]:
            f()
        rec_tile(kt, ucat)


def _inproj_fwd(x, g, w, b, perms, rec_params, d_rec):
    bsz, s, d = x.shape
    rows = SCAN_STEPS * SUBLANES
    halo_blocks = SCAN_STEPS // SUBLANES
    n_halo = s // SUBLANES

    def tm(width):
        return jax.ShapeDtypeStruct((s, bsz * width), BF16), pl.BlockSpec((SCAN_STEPS, bsz * width), lambda t: (t, 0))

    def sm(width):
        return jax.ShapeDtypeStruct((s * bsz, width), BF16), pl.BlockSpec((rows, width), lambda t: (t, 0))

    outs = [tm(D_ATT), tm(D_ATT), tm(D_ATT), tm(d), tm(d), sm(d_rec), sm(d_rec), sm(d_rec)]
    consts = [g, w, b, *perms, *rec_params]
    return pl.pallas_call(
        functools.partial(_inproj_fwd_kernel, d_rec=d_rec, d_model=d),
        grid=(s // SCAN_STEPS,),
        in_specs=[
            pl.BlockSpec((bsz, SUBLANES, d), lambda t: (0, jnp.maximum(t * halo_blocks - 1, 0), 0)),
            pl.BlockSpec((bsz, SCAN_STEPS, d), lambda t: (0, t, 0)),
            pl.BlockSpec((bsz, SUBLANES, d), lambda t: (0, jnp.minimum((t + 1) * halo_blocks, n_halo - 1), 0)),
        ] + [_const_spec(c.shape) for c in consts],
        out_specs=[o[1] for o in outs],
        out_shape=[o[0] for o in outs],
        scratch_shapes=[pltpu.VMEM((SUBLANES, d_rec), F32),
                        pltpu.VMEM((d // LANES, SUBLANES * PERM_PITCH, LANES), F32)],
        compiler_params=pltpu.CompilerParams(dimension_semantics=("arbitrary",), vmem_limit_bytes=VMEM_LIMIT),
        name="inproj_rglru_fwd",
    )(x, x, x, *consts)


def _attn_kernel(q_ref, k0_ref, k1_ref, k2_ref, v0_ref, v1_ref, v2_ref, tlo_ref, thi_ref, o_ref, bias_ref,
                 *, n_rows):
    band = pl.program_id(0)
    rows_q = ATT_BAND * GRID_W
    lane_row = lax.broadcasted_iota(jnp.int32, (GRID_W, LANES), 1) < HEAD_DIM

    @pl.when(pl.program_id(1) == 0)
    def _():
        r0 = band * ATT_BAND
        k0 = jnp.clip(r0 - WIN_H // 2, 0, n_rows - ATT_KEY_ROWS)
        for i in range(ATT_BAND):
            rq = r0 + i
            rs = jnp.clip(rq - WIN_H // 2, 0, n_rows - WIN_H)
            for jp in range(ATT_KEY_ROWS // HEADS_PER_VREG):
                rka = k0 + 2 * jp
                rkb = rka + 1
                va = jnp.logical_and(rka >= rs, rka < rs + WIN_H)
                vb = jnp.logical_and(rkb >= rs, rkb < rs + WIN_H)
                dra = jnp.clip(rka - rq + (WIN_H - 1), 0, 2 * WIN_H - 2)
                drb = jnp.clip(rkb - rq + (WIN_H - 1), 0, 2 * WIN_H - 2)
                for h in range(N_HEADS):
                    a = jnp.where(va, tlo_ref[h, dra], NEG)
                    b = jnp.where(vb, thi_ref[h, drb], NEG)
                    bias_ref[h, i * GRID_W:(i + 1) * GRID_W, jp * LANES:(jp + 1) * LANES] = jnp.where(lane_row, a, b)

    first = lax.broadcasted_iota(jnp.int32, (rows_q, LANES), 1) < HEAD_DIM
    k_refs = (k0_ref, k1_ref, k2_ref)
    v_refs = (v0_ref, v1_ref, v2_ref)
    zero = jnp.zeros((), BF16)

    def scores(hd):
        j, half = divmod(hd, HEADS_PER_VREG)
        cs = slice(j * LANES, (j + 1) * LANES)
        qh = jnp.where(first if half == 0 else jnp.logical_not(first), q_ref[:, cs], zero)
        s = jnp.concatenate(
            [lax.dot_general(qh, r[:, cs], (((1,), (1,)), ((), ())), preferred_element_type=F32) for r in k_refs],
            axis=1) + bias_ref[hd]
        return s, jnp.max(s, axis=-1, keepdims=True)

    nxt = scores(0)
    outs = []
    for hd in range(N_HEADS):
        j = hd // HEADS_PER_VREG
        cs = slice(j * LANES, (j + 1) * LANES)
        s, m = nxt
        if hd + 1 < N_HEADS:
            nxt = scores(hd + 1)
        p = jnp.exp2(s - m)
        l = jnp.sum(p, axis=-1, keepdims=True)
        pb = p.astype(BF16)
        o = jnp.dot(pb[:, 0:MXU_TILE], v_refs[0][:, cs], preferred_element_type=F32)
        for jj in range(1, len(v_refs)):
            o = o + jnp.dot(pb[:, jj * MXU_TILE:(jj + 1) * MXU_TILE], v_refs[jj][:, cs], preferred_element_type=F32)
        outs.append(o / l)
        if len(outs) == HEADS_PER_VREG:
            o_ref[:, cs] = jnp.where(first, outs[0], outs[1]).astype(BF16)
            outs = []


def _attention(q, k, v, tlo, thi, bsz):
    s = q.shape[0]
    n_rows = s // GRID_W
    n_bands = n_rows // ATT_BAND
    rows_q = ATT_BAND * GRID_W
    n_kblk = ATT_KEY_ROWS * GRID_W // MXU_TILE
    assert rows_q == MXU_TILE and ATT_KEY_ROWS * GRID_W == n_kblk * MXU_TILE

    def kv_spec(jj):
        def idx(band, bi):
            return (jnp.clip(band - 1, 0, n_bands - n_kblk) + jj, bi)
        return pl.BlockSpec((MXU_TILE, D_ATT), idx)

    return pl.pallas_call(
        functools.partial(_attn_kernel, n_rows=n_rows),
        grid=(n_bands, bsz),
        in_specs=[pl.BlockSpec((rows_q, D_ATT), lambda band, bi: (band, bi))]
        + [kv_spec(jj) for jj in range(n_kblk)] * 2
        + [_const_spec(tlo.shape), _const_spec(thi.shape)],
        out_specs=pl.BlockSpec((rows_q, D_ATT), lambda band, bi: (band, bi)),
        out_shape=jax.ShapeDtypeStruct((s, bsz * D_ATT), BF16),
        scratch_shapes=[pltpu.VMEM((N_HEADS, rows_q, ATT_KEY_ROWS * GRID_W), F32)],
        compiler_params=pltpu.CompilerParams(
            dimension_semantics=("arbitrary", "arbitrary"), vmem_limit_bytes=VMEM_LIMIT),
        name="nbr_attention",
    )(q, k, k, k, v, v, v, tlo, thi)


def _bias_tables(rpb):
    qc = np.arange(GRID_W)[:, None]
    kc = np.arange(GRID_W)[None, :]
    ws = np.clip(qc - WIN_W // 2, 0, GRID_W - WIN_W)
    valid = (kc >= ws) & (kc < ws + WIN_W)
    d_col = np.clip(kc - qc, -(WIN_W - 1), WIN_W - 1) + (WIN_W - 1)
    onehot = (np.arange(2 * WIN_W - 1)[:, None, None] == d_col[None]).astype(np.float32)
    t = jnp.einsum("hrd,dqk->hrqk", rpb.astype(F32), onehot, precision=lax.Precision.HIGHEST)
    t = jnp.where(valid[None, None], t * LOG2E, NEG)
    neg = jnp.full_like(t, NEG)
    return jnp.concatenate([t, neg], axis=-1), jnp.concatenate([neg, t], axis=-1)


def _rglru_tile(ucat, kt, cw_ref, cb_ref, wa_ref, ba_ref, wi_ref, bi_ref, lam_ref, carry, reverse):
    rows = SCAN_STEPS * SUBLANES
    cs = slice(kt * MXU_TILE, (kt + 1) * MXU_TILE)
    conv = cb_ref[:, cs]
    for j in range(CONV_W):
        conv = conv + ucat[j * SUBLANES:j * SUBLANES + rows] * cw_ref[j:j + 1, cs]
    conv_b = conv.astype(BF16)

    def gate(w_ref, b_ref):
        return _sigmoid(jnp.dot(conv_b, w_ref[kt], preferred_element_type=F32) + b_ref[:, cs])

    r_gate = gate(wa_ref, ba_ref)
    i_gate = gate(wi_ref, bi_ref)
    log_a = r_gate * (-LRU_C * _softplus(-lam_ref[:, cs]))
    a = jnp.exp(log_a)
    z = jnp.maximum(jnp.tanh(log_a) * (-1.0 - a * a), 0.0)
    mult = jnp.where(z > 0.0, z * lax.rsqrt(z), 0.0)
    bx = mult * (i_gate * conv)

    h = carry[:, cs]
    hs = [None] * SCAN_STEPS
    for i in range(SCAN_STEPS):
        t = SCAN_STEPS - 1 - i if reverse else i
        sl = slice(t * SUBLANES, (t + 1) * SUBLANES)
        h = a[sl] * h + bx[sl]
        hs[t] = h
    carry[:, cs] = h
    return jnp.concatenate(hs, axis=0)


def _tail_kernel(uprev_ref, u_ref, unext_ref, cw_ref, cb_ref, wa_ref, ba_ref, wi_ref, bi_ref, lam_ref,
                 gy_ref, hf_ref, x_ref, att_ref, sga_ref, sgr_ref, wao_ref, wro_ref, wout_ref,
                 g2_ref, w1_ref, w2_ref, gf_ref, o_ref, carry, y_s, *, ff_chunk):
    j = pl.program_id(0)
    n_t = pl.num_programs(0) - 1
    rows = SCAN_STEPS * SUBLANES
    d = x_ref.shape[-1]
    prev_rows = (CONV_W // 2) * SUBLANES

    @pl.when(j == 0)
    def _():
        carry[...] = jnp.zeros_like(carry)
        y_s[...] = jnp.zeros_like(y_s)

    t_idx = n_t - 1 - jnp.minimum(j, n_t - 1)
    y_nat = jnp.concatenate(
        [jnp.concatenate([y_s[c, b * PERM_PITCH:b * PERM_PITCH + SCAN_STEPS, :] for c in range(d // LANES)], axis=1)
         for b in range(SUBLANES)], axis=0).astype(BF16)
    n_tiles = d // MXU_TILE

    def rec_tile(kt):
        cs = slice(kt * MXU_TILE, (kt + 1) * MXU_TILE)
        prev = jnp.where(t_idx == 0, 0.0, uprev_ref[:, cs].astype(F32)[BF16_ROWS - prev_rows:])
        after = jnp.where(t_idx == n_t - 1, 0.0, unext_ref[:, cs].astype(F32)[0:SUBLANES])
        ucat = jnp.concatenate([prev, u_ref[:, cs].astype(F32), after], axis=0)
        h_bwd = _rglru_tile(ucat, kt, cw_ref, cb_ref, wa_ref, ba_ref, wi_ref, bi_ref, lam_ref, carry, reverse=True)
        y = (hf_ref[:, cs].astype(F32) + h_bwd) * gy_ref[:, cs].astype(F32)
        for ts in range(SCAN_STEPS):
            for ci in range(MXU_TILE // LANES):
                y_s[kt * (MXU_TILE // LANES) + ci, pl.ds(ts, SUBLANES, stride=PERM_PITCH), :] = (
                    y[ts * SUBLANES:(ts + 1) * SUBLANES, ci * LANES:(ci + 1) * LANES])

    y_rec = jnp.dot(y_nat, wro_ref[...], preferred_element_type=F32)
    y_att = jnp.dot(_stack_batches(att_ref, D_ATT), wao_ref[...], preferred_element_type=F32)
    mixed = _stack_batches(sga_ref, d).astype(F32) * y_att + _stack_batches(sgr_ref, d).astype(F32) * y_rec
    out = x_ref[...].reshape(rows, d) + jnp.dot(mixed.astype(BF16), wout_ref[...], preferred_element_type=F32)

    h2 = _rms(out, g2_ref[...]).astype(BF16)
    d_ff = w1_ref.shape[1]
    n_chunks = d_ff // ff_chunk
    for c in range(max(n_chunks, n_tiles)):
        if c < n_chunks:
            fs = slice(c * ff_chunk, (c + 1) * ff_chunk)
            hid = jnp.maximum(jnp.dot(h2, w1_ref[:, fs], preferred_element_type=F32), 0.0)
            out = out + jnp.dot((hid * hid).astype(BF16), w2_ref[fs, :], preferred_element_type=F32)
        if c < n_tiles:
            rec_tile(c)
    o_ref[...] = _rms(out, gf_ref[...]).reshape(o_ref.shape)


def _tail(u, params, gy, h_fwd, x, att, sga, sgr, wao, wro, wout, g2, w1, w2, gf):
    n_rows, c = u.shape
    bsz, s, d = x.shape
    rows = SCAN_STEPS * SUBLANES
    n_t = s // SCAN_STEPS
    halo_per_block = rows // BF16_ROWS
    n_halo = n_rows // BF16_ROWS

    def scan_blk(j):
        return n_t - 1 - jnp.minimum(j, n_t - 1)

    def merge_blk(j):
        return n_t - 1 - jnp.maximum(j - 1, 0)

    sm_spec = pl.BlockSpec((rows, c), lambda j: (scan_blk(j), 0))
    halo_specs = [
        pl.BlockSpec((BF16_ROWS, c), lambda j: (jnp.maximum(scan_blk(j) * halo_per_block - 1, 0), 0)),
        sm_spec,
        pl.BlockSpec((BF16_ROWS, c), lambda j: (jnp.minimum((scan_blk(j) + 1) * halo_per_block, n_halo - 1), 0)),
    ]

    def tm(width):
        return pl.BlockSpec((SCAN_STEPS, bsz * width), lambda j: (merge_blk(j), 0))

    x_spec = pl.BlockSpec((bsz, SCAN_STEPS, d), lambda j: (0, merge_blk(j), 0))
    weights = [wao, wro, wout, g2, w1, w2, gf]
    return pl.pallas_call(
        functools.partial(_tail_kernel, ff_chunk=FF_CHUNK),
        grid=(n_t + 1,),
        in_specs=halo_specs + [_const_spec(p.shape) for p in params] + [sm_spec, sm_spec]
        + [x_spec, tm(D_ATT), tm(d), tm(d)]
        + [_const_spec(p.shape) for p in weights],
        out_specs=x_spec,
        out_shape=jax.ShapeDtypeStruct(x.shape, F32),
        scratch_shapes=[pltpu.VMEM((SUBLANES, c), F32), pltpu.VMEM((c // LANES, SUBLANES * PERM_PITCH, LANES), F32)],
        compiler_params=pltpu.CompilerParams(dimension_semantics=("arbitrary",), vmem_limit_bytes=VMEM_LIMIT_TAIL),
        name="rglru_bwd_merge_ffn",
    )(u, u, u, *params, gy, h_fwd, x, att, sga, sgr, *weights)


def _gate_tiles(w):
    nb, bw, _ = w.shape
    per = MXU_TILE // bw
    eye = jnp.eye(per, dtype=w.dtype)
    t = jnp.einsum("cpij,pq->cpiqj", w.reshape(nb // per, per, bw, bw), eye)
    return t.reshape(nb // per, MXU_TILE, MXU_TILE).astype(BF16)


def kernel(x, ln1_g, w_in, b_in, rpb, w_att_o, conv_w, conv_b, w_rg_a, b_rg_a, w_rg_i, b_rg_i, lru_lambda,
           w_rec_o, w_out, ln2_g, w_ff1, w_ff2, lnf_g):
    bsz, s, d = x.shape
    d_rec = conv_w.shape[-1]
    assert ln1_g.shape[0] == 1, "single-layer stack only"
    assert bsz == SUBLANES and s % SCAN_STEPS == 0 and s % (ATT_BAND * GRID_W) == 0 and w_ff1.shape[-1] % FF_CHUNK == 0
    assert w_in.shape[-1] == 3 * D_ATT + 2 * d_rec + 2 * d and d_rec % MXU_TILE == 0 and d_rec == d
    row = lambda p: p.reshape(1, -1).astype(F32)
    l = 0

    def dir_params(di):
        return [conv_w[l].astype(F32), row(conv_b[l]), _gate_tiles(w_rg_a[l, di]), row(b_rg_a[l, di]),
                _gate_tiles(w_rg_i[l, di]), row(b_rg_i[l, di]), row(lru_lambda[l, di])]

    perms = [jnp.asarray(m, BF16) for m in _halo_perm_matrices()]
    q, k, v, sga, sgr, u, gy, h_fwd = _inproj_fwd(x, row(ln1_g[l]), w_in[l].astype(BF16), row(b_in[l]), perms,
                                                  dir_params(0), d_rec)

    tlo, thi = _bias_tables(rpb[l])
    att = _attention(q, k, v, tlo, thi, bsz)

    return _tail(u, dir_params(1), gy, h_fwd, x, att, sga, sgr,
                 w_att_o[l].astype(BF16), w_rec_o[l].astype(BF16), w_out[l].astype(BF16),
                 row(ln2_g[l]), w_ff1[l].astype(BF16), w_ff2[l].astype(BF16), row(lnf_g))
```

```python
import functools

import jax
import jax.numpy as jnp
import numpy as np
from jax import lax
from jax.experimental import pallas as pl
from jax.experimental.pallas import tpu as pltpu

F32 = jnp.float32
BF16 = jnp.bfloat16

GRID_W = 64
N_HEADS = 8
HEAD_DIM = 64
D_ATT = N_HEADS * HEAD_DIM
WIN_H = 8
WIN_W = 16
CONV_W = 4
LRU_C = 8.0
EPS = 1e-6
NEG = -1e30
LOG2E = float(np.log2(np.e))

LANES = 128
SUBLANES = 8
BF16_ROWS = 16
MXU_TILE = 256
VMEM_LIMIT = 56 * 1024 * 1024
VMEM_LIMIT_TAIL = 60 * 1024 * 1024

PROJ_TILE = 256
FF_CHUNK = 1024
SCAN_STEPS = 64
PERM_PITCH = SCAN_STEPS + SUBLANES
ATT_BAND = 4
ATT_KEY_ROWS = 12
HEADS_PER_VREG = LANES // HEAD_DIM


def _rms(x, g):
    ms = jnp.mean(x * x, axis=-1, keepdims=True)
    return x * lax.rsqrt(ms + EPS) * g


def _sigmoid(x):
    return 1.0 / (1.0 + jnp.exp2(x * (-LOG2E)))


def _gelu_tanh(x):
    c = float(np.sqrt(2.0 / np.pi))
    return 0.5 * x * (1.0 + jnp.tanh(c * (x + 0.044715 * (x * x * x))))


def _softplus(x):
    return jnp.maximum(x, 0.0) + jnp.log1p(jnp.exp(-jnp.abs(x)))


def _const_spec(shape):
    nd = len(shape)
    return pl.BlockSpec(shape, lambda *_: (0,) * nd, pipeline_mode=pl.Buffered(1))


def _stack_batches(ref, c, lo=0, width=None):
    width = c if width is None else width
    return jnp.concatenate([ref[:, b * c + lo:b * c + lo + width] for b in range(SUBLANES)], axis=0)


def _store_batches(ref, val, c, lo=0):
    steps = ref.shape[0]
    width = val.shape[1]
    for b in range(SUBLANES):
        ref[:, b * c + lo:b * c + lo + width] = val[b * steps:(b + 1) * steps].astype(ref.dtype)


def _halo_perm_matrices():
    prev_steps = CONV_W // 2
    halo_rows = SUBLANES * SUBLANES
    pp = np.zeros((prev_steps * SUBLANES, halo_rows), np.float32)
    pn = np.zeros((SUBLANES, halo_rows), np.float32)
    for b in range(SUBLANES):
        for j in range(prev_steps):
            pp[j * SUBLANES + b, b * SUBLANES + SUBLANES - prev_steps + j] = 1.0
        pn[b, b * SUBLANES] = 1.0
    return pp, pn


def _inproj_fwd_kernel(xprev_ref, x_ref, xnext_ref, g_ref, w_ref, b_ref, pp_ref, pn_ref,
                       cw_ref, cb_ref, wa_ref, ba_ref, wi_ref, bi_ref, lam_ref,
                       q_ref, k_ref, v_ref, sga_ref, sgr_ref, u_ref, gy_ref, hf_ref, carry, perm_s, wa_t, wi_t,
                       *, d_rec, d_model):
    t = pl.program_id(0)
    n_t = pl.num_programs(0)
    rows = SCAN_STEPS * SUBLANES
    prev_rows = (CONV_W // 2) * SUBLANES

    @pl.when(t == 0)
    def _():
        carry[...] = jnp.zeros_like(carry)
        _build_gate_tiles(wa_ref, wa_t)
        _build_gate_tiles(wi_ref, wi_t)

    def normed(ref):
        return _rms(ref[...].reshape(-1, d_model), g_ref[...])

    def permuted(p, hb):
        return jnp.dot(p[...], hb, preferred_element_type=F32).astype(BF16)

    def proj(lhs, lo, width):
        return jnp.dot(lhs, w_ref[:, lo:lo + width], preferred_element_type=F32) + b_ref[:, lo:lo + width]

    def step_major(hb):
        for c in range(d_model // LANES):
            for b in range(SUBLANES):
                perm_s[c, b * PERM_PITCH:b * PERM_PITCH + SCAN_STEPS, :] = (
                    hb[b * SCAN_STEPS:(b + 1) * SCAN_STEPS, c * LANES:(c + 1) * LANES])
        return jnp.concatenate(
            [jnp.concatenate([perm_s[c, pl.ds(ts, SUBLANES, stride=PERM_PITCH), :] for c in range(d_model // LANES)],
                             axis=1) for ts in range(SCAN_STEPS)], axis=0)

    h32 = normed(x_ref)
    h = h32.astype(BF16)
    hp = step_major(h32).astype(BF16)
    lhs_u = jnp.concatenate([permuted(pp_ref, normed(xprev_ref).astype(BF16)), hp,
                             permuted(pn_ref, normed(xnext_ref).astype(BF16))], axis=0)
    lo_u = 3 * D_ATT
    lo_y = lo_u + d_rec
    lo_ga = lo_y + d_rec
    lo_gr = lo_ga + d_model

    def u_tile(kt):
        return proj(lhs_u, lo_u + kt * MXU_TILE, MXU_TILE)

    def rec_tile(kt, ucat):
        cs = slice(kt * MXU_TILE, (kt + 1) * MXU_TILE)
        u_ref[:, cs] = ucat[prev_rows:prev_rows + rows].astype(BF16)
        ucat = jnp.concatenate([jnp.where(t == 0, 0.0, ucat[:prev_rows]), ucat[prev_rows:prev_rows + rows],
                                jnp.where(t == n_t - 1, 0.0, ucat[prev_rows + rows:])], axis=0)
        h_fwd = _rglru_tile(ucat, kt, cw_ref, cb_ref, wa_t, ba_ref, wi_t, bi_ref, lam_ref, carry, reverse=False)
        hf_ref[:, cs] = h_fwd.astype(BF16)

    def att_tile(ref, lo, kt, scale):
        val = proj(h, lo + kt * PROJ_TILE, PROJ_TILE)
        _store_batches(ref, val if scale is None else val * scale, D_ATT, kt * PROJ_TILE)

    def gy_tile(kt):
        cs = slice(kt * PROJ_TILE, (kt + 1) * PROJ_TILE)
        gy_ref[:, cs] = _gelu_tanh(proj(hp, lo_y + kt * PROJ_TILE, PROJ_TILE)).astype(BF16)

    def gate_tile(ref, lo, kt):
        _store_batches(ref, _sigmoid(proj(h, lo + kt * PROJ_TILE, PROJ_TILE)), d_model, kt * PROJ_TILE)

    fillers = [functools.partial(att_tile, q_ref, 0, kt, HEAD_DIM ** -0.5 * LOG2E) for kt in range(D_ATT // PROJ_TILE)]
    fillers += [functools.partial(att_tile, k_ref, D_ATT, kt, None) for kt in range(D_ATT // PROJ_TILE)]
    fillers += [functools.partial(att_tile, v_ref, 2 * D_ATT, kt, None) for kt in range(D_ATT // PROJ_TILE)]
    fillers += [functools.partial(gy_tile, kt) for kt in range(d_rec // PROJ_TILE)]
    fillers += [functools.partial(gate_tile, sga_ref, lo_ga, kt) for kt in range(d_model // PROJ_TILE)]
    fillers += [functools.partial(gate_tile, sgr_ref, lo_gr, kt) for kt in range(d_model // PROJ_TILE)]
    n_rec = d_rec // MXU_TILE
    per_rec = -(-len(fillers) // n_rec)
    ucat_next = u_tile(0)
    for kt in range(n_rec):
        ucat = ucat_next
        if kt + 1 < n_rec:
            ucat_next = u_tile(kt + 1)
        for f in fillers[kt * per_rec:(kt + 1) * per_rec]:
            f()
        rec_tile(kt, ucat)


def _inproj_fwd(x, g, w, b, perms, rec_params, d_rec):
    bsz, s, d = x.shape
    rows = SCAN_STEPS * SUBLANES
    halo_blocks = SCAN_STEPS // SUBLANES
    n_halo = s // SUBLANES

    def tm(width):
        return jax.ShapeDtypeStruct((s, bsz * width), BF16), pl.BlockSpec((SCAN_STEPS, bsz * width), lambda t: (t, 0))

    def sm(width):
        return jax.ShapeDtypeStruct((s * bsz, width), BF16), pl.BlockSpec((rows, width), lambda t: (t, 0))

    outs = [tm(D_ATT), tm(D_ATT), tm(D_ATT), tm(d), tm(d), sm(d_rec), sm(d_rec), sm(d_rec)]
    consts = [g, w, b, *perms, *rec_params]
    return pl.pallas_call(
        functools.partial(_inproj_fwd_kernel, d_rec=d_rec, d_model=d),
        grid=(s // SCAN_STEPS,),
        in_specs=[
            pl.BlockSpec((bsz, SUBLANES, d), lambda t: (0, jnp.maximum(t * halo_blocks - 1, 0), 0)),
            pl.BlockSpec((bsz, SCAN_STEPS, d), lambda t: (0, t, 0)),
            pl.BlockSpec((bsz, SUBLANES, d), lambda t: (0, jnp.minimum((t + 1) * halo_blocks, n_halo - 1), 0)),
        ] + [_const_spec(c.shape) for c in consts],
        out_specs=[o[1] for o in outs],
        out_shape=[o[0] for o in outs],
        scratch_shapes=[pltpu.VMEM((SUBLANES, d_rec), F32),
                        pltpu.VMEM((d // LANES, SUBLANES * PERM_PITCH, LANES), F32),
                        _gate_tile_scratch(rec_params[2]), _gate_tile_scratch(rec_params[4])],
        compiler_params=pltpu.CompilerParams(dimension_semantics=("arbitrary",), vmem_limit_bytes=VMEM_LIMIT),
        name="inproj_rglru_fwd",
    )(x, x, x, *consts)


def _attn_kernel(q_ref, k0_ref, k1_ref, k2_ref, v0_ref, v1_ref, v2_ref, tlo_ref, thi_ref, o_ref, bias_ref,
                 *, n_rows):
    band = pl.program_id(0)
    rows_q = ATT_BAND * GRID_W
    lane_row = lax.broadcasted_iota(jnp.int32, (GRID_W, LANES), 1) < HEAD_DIM

    @pl.when(pl.program_id(1) == 0)
    def _():
        r0 = band * ATT_BAND
        k0 = jnp.clip(r0 - WIN_H // 2, 0, n_rows - ATT_KEY_ROWS)
        for i in range(ATT_BAND):
            rq = r0 + i
            rs = jnp.clip(rq - WIN_H // 2, 0, n_rows - WIN_H)
            for jp in range(ATT_KEY_ROWS // HEADS_PER_VREG):
                rka = k0 + 2 * jp
                rkb = rka + 1
                va = jnp.logical_and(rka >= rs, rka < rs + WIN_H)
                vb = jnp.logical_and(rkb >= rs, rkb < rs + WIN_H)
                dra = jnp.clip(rka - rq + (WIN_H - 1), 0, 2 * WIN_H - 2)
                drb = jnp.clip(rkb - rq + (WIN_H - 1), 0, 2 * WIN_H - 2)
                for h in range(N_HEADS):
                    a = jnp.where(va, tlo_ref[h, dra], NEG)
                    b = jnp.where(vb, thi_ref[h, drb], NEG)
                    bias_ref[h, i * GRID_W:(i + 1) * GRID_W, jp * LANES:(jp + 1) * LANES] = jnp.where(lane_row, a, b)

    first = lax.broadcasted_iota(jnp.int32, (rows_q, LANES), 1) < HEAD_DIM
    k_refs = (k0_ref, k1_ref, k2_ref)
    v_refs = (v0_ref, v1_ref, v2_ref)
    zero = jnp.zeros((), BF16)

    def scores(hd):
        j, half = divmod(hd, HEADS_PER_VREG)
        cs = slice(j * LANES, (j + 1) * LANES)
        qh = jnp.where(first if half == 0 else jnp.logical_not(first), q_ref[:, cs], zero)
        s = jnp.concatenate(
            [lax.dot_general(qh, r[:, cs], (((1,), (1,)), ((), ())), preferred_element_type=F32) for r in k_refs],
            axis=1) + bias_ref[hd]
        return s, jnp.max(s, axis=-1, keepdims=True)

    nxt = scores(0)
    outs = []
    for hd in range(N_HEADS):
        j = hd // HEADS_PER_VREG
        cs = slice(j * LANES, (j + 1) * LANES)
        s, m = nxt
        if hd + 1 < N_HEADS:
            nxt = scores(hd + 1)
        p = jnp.exp2(s - m)
        l = jnp.sum(p, axis=-1, keepdims=True)
        pb = p.astype(BF16)
        o = jnp.dot(pb[:, 0:MXU_TILE], v_refs[0][:, cs], preferred_element_type=F32)
        for jj in range(1, len(v_refs)):
            o = o + jnp.dot(pb[:, jj * MXU_TILE:(jj + 1) * MXU_TILE], v_refs[jj][:, cs], preferred_element_type=F32)
        outs.append(o / l)
        if len(outs) == HEADS_PER_VREG:
            o_ref[:, cs] = jnp.where(first, outs[0], outs[1]).astype(BF16)
            outs = []


def _attention(q, k, v, tlo, thi, bsz):
    s = q.shape[0]
    n_rows = s // GRID_W
    n_bands = n_rows // ATT_BAND
    rows_q = ATT_BAND * GRID_W
    n_kblk = ATT_KEY_ROWS * GRID_W // MXU_TILE
    assert rows_q == MXU_TILE and ATT_KEY_ROWS * GRID_W == n_kblk * MXU_TILE

    def kv_spec(jj):
        def idx(band, bi):
            return (jnp.clip(band - 1, 0, n_bands - n_kblk) + jj, bi)
        return pl.BlockSpec((MXU_TILE, D_ATT), idx)

    return pl.pallas_call(
        functools.partial(_attn_kernel, n_rows=n_rows),
        grid=(n_bands, bsz),
        in_specs=[pl.BlockSpec((rows_q, D_ATT), lambda band, bi: (band, bi))]
        + [kv_spec(jj) for jj in range(n_kblk)] * 2
        + [_const_spec(tlo.shape), _const_spec(thi.shape)],
        out_specs=pl.BlockSpec((rows_q, D_ATT), lambda band, bi: (band, bi)),
        out_shape=jax.ShapeDtypeStruct((s, bsz * D_ATT), BF16),
        scratch_shapes=[pltpu.VMEM((N_HEADS, rows_q, ATT_KEY_ROWS * GRID_W), F32)],
        compiler_params=pltpu.CompilerParams(
            dimension_semantics=("arbitrary", "arbitrary"), vmem_limit_bytes=VMEM_LIMIT),
        name="nbr_attention",
    )(q, k, k, k, v, v, v, tlo, thi)


def _bias_tables(rpb):
    qc = np.arange(GRID_W)[:, None]
    kc = np.arange(GRID_W)[None, :]
    ws = np.clip(qc - WIN_W // 2, 0, GRID_W - WIN_W)
    valid = (kc >= ws) & (kc < ws + WIN_W)
    d_col = np.clip(kc - qc, -(WIN_W - 1), WIN_W - 1) + (WIN_W - 1)
    onehot = (np.arange(2 * WIN_W - 1)[:, None, None] == d_col[None]).astype(np.float32)
    t = jnp.einsum("hrd,dqk->hrqk", rpb.astype(F32), onehot, precision=lax.Precision.HIGHEST)
    t = jnp.where(valid[None, None], t * LOG2E, NEG)
    neg = jnp.full_like(t, NEG)
    return jnp.concatenate([t, neg], axis=-1), jnp.concatenate([neg, t], axis=-1)


def _rglru_tile(ucat, kt, cw_ref, cb_ref, wa_ref, ba_ref, wi_ref, bi_ref, lam_ref, carry, reverse):
    rows = SCAN_STEPS * SUBLANES
    cs = slice(kt * MXU_TILE, (kt + 1) * MXU_TILE)
    conv = cb_ref[:, cs]
    for j in range(CONV_W):
        conv = conv + ucat[j * SUBLANES:j * SUBLANES + rows] * cw_ref[j:j + 1, cs]
    conv_b = conv.astype(BF16)

    def gate(w_ref, b_ref):
        return _sigmoid(jnp.dot(conv_b, w_ref[kt], preferred_element_type=F32) + b_ref[:, cs])

    r_gate = gate(wa_ref, ba_ref)
    i_gate = gate(wi_ref, bi_ref)
    log_a = r_gate * (-LRU_C * _softplus(-lam_ref[:, cs]))
    a = jnp.exp(log_a)
    z = jnp.maximum(jnp.tanh(log_a) * (-1.0 - a * a), 0.0)
    mult = jnp.where(z > 0.0, z * lax.rsqrt(z), 0.0)
    bx = mult * (i_gate * conv)

    h = carry[:, cs]
    hs = [None] * SCAN_STEPS
    for i in range(SCAN_STEPS):
        t = SCAN_STEPS - 1 - i if reverse else i
        sl = slice(t * SUBLANES, (t + 1) * SUBLANES)
        h = a[sl] * h + bx[sl]
        hs[t] = h
    carry[:, cs] = h
    return jnp.concatenate(hs, axis=0)


def _tail_kernel(uprev_ref, u_ref, unext_ref, cw_ref, cb_ref, wa_ref, ba_ref, wi_ref, bi_ref, lam_ref,
                 gy_ref, hf_ref, x_ref, att_ref, sga_ref, sgr_ref, wao_ref, wro_ref, wout_ref,
                 g2_ref, w1_ref, w2_ref, gf_ref, o_ref, carry, y_s, wa_t, wi_t, *, ff_chunk):
    j = pl.program_id(0)
    n_t = pl.num_programs(0) - 1
    rows = SCAN_STEPS * SUBLANES
    d = x_ref.shape[-1]
    prev_rows = (CONV_W // 2) * SUBLANES
    t_idx = n_t - 1 - jnp.minimum(j, n_t - 1)
    n_tiles = d // MXU_TILE

    def rec_tile(kt):
        cs = slice(kt * MXU_TILE, (kt + 1) * MXU_TILE)
        prev = jnp.where(t_idx == 0, 0.0, uprev_ref[:, cs].astype(F32)[BF16_ROWS - prev_rows:])
        after = jnp.where(t_idx == n_t - 1, 0.0, unext_ref[:, cs].astype(F32)[0:SUBLANES])
        ucat = jnp.concatenate([prev, u_ref[:, cs].astype(F32), after], axis=0)
        h_bwd = _rglru_tile(ucat, kt, cw_ref, cb_ref, wa_t, ba_ref, wi_t, bi_ref, lam_ref, carry, reverse=True)
        y = (hf_ref[:, cs].astype(F32) + h_bwd) * gy_ref[:, cs].astype(F32)
        for ts in range(SCAN_STEPS):
            for ci in range(MXU_TILE // LANES):
                y_s[kt * (MXU_TILE // LANES) + ci, pl.ds(ts, SUBLANES, stride=PERM_PITCH), :] = (
                    y[ts * SUBLANES:(ts + 1) * SUBLANES, ci * LANES:(ci + 1) * LANES])

    @pl.when(j == 0)
    def _():
        carry[...] = jnp.zeros_like(carry)
        _build_gate_tiles(wa_ref, wa_t)
        _build_gate_tiles(wi_ref, wi_t)
        for kt in range(n_tiles):
            rec_tile(kt)

    @pl.when(j > 0)
    def _():
        y_nat = jnp.concatenate(
            [jnp.concatenate([y_s[c, b * PERM_PITCH:b * PERM_PITCH + SCAN_STEPS, :] for c in range(d // LANES)],
                             axis=1) for b in range(SUBLANES)], axis=0).astype(BF16)
        y_rec = jnp.dot(y_nat, wro_ref[...], preferred_element_type=F32)
        y_att = jnp.dot(_stack_batches(att_ref, D_ATT), wao_ref[...], preferred_element_type=F32)
        mixed = _stack_batches(sga_ref, d).astype(F32) * y_att + _stack_batches(sgr_ref, d).astype(F32) * y_rec
        out = x_ref[...].reshape(rows, d) + jnp.dot(mixed.astype(BF16), wout_ref[...], preferred_element_type=F32)

        h2 = _rms(out, g2_ref[...]).astype(BF16)
        d_ff = w1_ref.shape[1]
        n_chunks = d_ff // ff_chunk
        for c in range(max(n_chunks, n_tiles)):
            if c < n_chunks:
                fs = slice(c * ff_chunk, (c + 1) * ff_chunk)
                hid = jnp.maximum(jnp.dot(h2, w1_ref[:, fs], preferred_element_type=F32), 0.0)
                out = out + jnp.dot((hid * hid).astype(BF16), w2_ref[fs, :], preferred_element_type=F32)
            if c < n_tiles:
                rec_tile(c)
        o_ref[...] = _rms(out, gf_ref[...]).reshape(o_ref.shape)


def _tail(u, params, gy, h_fwd, x, att, sga, sgr, wao, wro, wout, g2, w1, w2, gf):
    n_rows, c = u.shape
    bsz, s, d = x.shape
    rows = SCAN_STEPS * SUBLANES
    n_t = s // SCAN_STEPS
    halo_per_block = rows // BF16_ROWS
    n_halo = n_rows // BF16_ROWS

    def scan_blk(j):
        return n_t - 1 - jnp.minimum(j, n_t - 1)

    def merge_blk(j):
        return n_t - 1 - jnp.maximum(j - 1, 0)

    sm_spec = pl.BlockSpec((rows, c), lambda j: (scan_blk(j), 0))
    halo_specs = [
        pl.BlockSpec((BF16_ROWS, c), lambda j: (jnp.maximum(scan_blk(j) * halo_per_block - 1, 0), 0)),
        sm_spec,
        pl.BlockSpec((BF16_ROWS, c), lambda j: (jnp.minimum((scan_blk(j) + 1) * halo_per_block, n_halo - 1), 0)),
    ]

    def tm(width):
        return pl.BlockSpec((SCAN_STEPS, bsz * width), lambda j: (merge_blk(j), 0))

    x_spec = pl.BlockSpec((bsz, SCAN_STEPS, d), lambda j: (0, merge_blk(j), 0))
    weights = [wao, wro, wout, g2, w1, w2, gf]
    return pl.pallas_call(
        functools.partial(_tail_kernel, ff_chunk=FF_CHUNK),
        grid=(n_t + 1,),
        in_specs=halo_specs + [_const_spec(p.shape) for p in params] + [sm_spec, sm_spec]
        + [x_spec, tm(D_ATT), tm(d), tm(d)]
        + [_const_spec(p.shape) for p in weights],
        out_specs=x_spec,
        out_shape=jax.ShapeDtypeStruct(x.shape, F32),
        scratch_shapes=[pltpu.VMEM((SUBLANES, c), F32), pltpu.VMEM((c // LANES, SUBLANES * PERM_PITCH, LANES), F32),
                        _gate_tile_scratch(params[2]), _gate_tile_scratch(params[4])],
        compiler_params=pltpu.CompilerParams(dimension_semantics=("arbitrary",), vmem_limit_bytes=VMEM_LIMIT_TAIL),
        name="rglru_bwd_merge_ffn",
    )(u, u, u, *params, gy, h_fwd, x, att, sga, sgr, *weights)


def _build_gate_tiles(w_ref, tile_ref):
    n_blocks, bw, _ = w_ref.shape
    per = MXU_TILE // bw
    tile_ref[...] = jnp.zeros_like(tile_ref)
    for n in range(n_blocks):
        kt, p = divmod(n, per)
        tile_ref[kt, p * bw:(p + 1) * bw, p * bw:(p + 1) * bw] = w_ref[n].astype(tile_ref.dtype)


def _gate_tile_scratch(w):
    n_blocks, bw, _ = w.shape
    return pltpu.VMEM((n_blocks * bw // MXU_TILE, MXU_TILE, MXU_TILE), BF16)


def kernel(x, ln1_g, w_in, b_in, rpb, w_att_o, conv_w, conv_b, w_rg_a, b_rg_a, w_rg_i, b_rg_i, lru_lambda,
           w_rec_o, w_out, ln2_g, w_ff1, w_ff2, lnf_g):
    bsz, s, d = x.shape
    d_rec = conv_w.shape[-1]
    assert ln1_g.shape[0] == 1, "single-layer stack only"
    assert bsz == SUBLANES and s % SCAN_STEPS == 0 and s % (ATT_BAND * GRID_W) == 0 and w_ff1.shape[-1] % FF_CHUNK == 0
    assert w_in.shape[-1] == 3 * D_ATT + 2 * d_rec + 2 * d and d_rec % MXU_TILE == 0 and d_rec == d
    row = lambda p: p.reshape(1, -1).astype(F32)
    l = 0

    def dir_params(di):
        return [conv_w[l].astype(F32), row(conv_b[l]), w_rg_a[l, di].astype(F32), row(b_rg_a[l, di]),
                w_rg_i[l, di].astype(F32), row(b_rg_i[l, di]), row(lru_lambda[l, di])]

    perms = [jnp.asarray(m, BF16) for m in _halo_perm_matrices()]
    q, k, v, sga, sgr, u, gy, h_fwd = _inproj_fwd(x, row(ln1_g[l]), w_in[l].astype(BF16), row(b_in[l]), perms,
                                                  dir_params(0), d_rec)

    tlo, thi = _bias_tables(rpb[l])
    att = _attention(q, k, v, tlo, thi, bsz)

    return _tail(u, dir_params(1), gy, h_fwd, x, att, sga, sgr,
                 w_att_o[l].astype(BF16), w_rec_o[l].astype(BF16), w_out[l].astype(BF16),
                 row(ln2_g[l]), w_ff1[l].astype(BF16), w_ff2[l].astype(BF16), row(lnf_g))
```

```python
import functools

import jax
import jax.numpy as jnp
import numpy as np
from jax import lax
from jax.experimental import pallas as pl
from jax.experimental.pallas import tpu as pltpu

F32 = jnp.float32
BF16 = jnp.bfloat16

GRID_W = 64
N_HEADS = 8
HEAD_DIM = 64
D_ATT = N_HEADS * HEAD_DIM
WIN_H = 8
WIN_W = 16
CONV_W = 4
LRU_C = 8.0
EPS = 1e-6
NEG = -1e30
LOG2E = float(np.log2(np.e))

LANES = 128
SUBLANES = 8
BF16_ROWS = 16
MXU_TILE = 256
VMEM_LIMIT = 56 * 1024 * 1024
VMEM_LIMIT_TAIL = 60 * 1024 * 1024

PROJ_TILE = 256
FF_CHUNK = 1024
SCAN_STEPS = 64
PERM_PITCH = SCAN_STEPS + SUBLANES
ATT_BAND = 4
ATT_KEY_ROWS = 12
HEADS_PER_VREG = LANES // HEAD_DIM


def _rms(x, g):
    ms = jnp.mean(x * x, axis=-1, keepdims=True)
    return x * lax.rsqrt(ms + EPS) * g


def _sigmoid(x):
    return 1.0 / (1.0 + jnp.exp2(x * (-LOG2E)))


def _gelu_tanh(x):
    c = float(np.sqrt(2.0 / np.pi))
    return 0.5 * x * (1.0 + jnp.tanh(c * (x + 0.044715 * (x * x * x))))


def _softplus(x):
    return jnp.maximum(x, 0.0) + jnp.log1p(jnp.exp(-jnp.abs(x)))


def _const_spec(shape):
    nd = len(shape)
    return pl.BlockSpec(shape, lambda *_: (0,) * nd, pipeline_mode=pl.Buffered(1))


def _stack_batches(ref, c, lo=0, width=None):
    width = c if width is None else width
    return jnp.concatenate([ref[:, b * c + lo:b * c + lo + width] for b in range(SUBLANES)], axis=0)


def _store_batches(ref, val, c, lo=0):
    steps = ref.shape[0]
    width = val.shape[1]
    for b in range(SUBLANES):
        ref[:, b * c + lo:b * c + lo + width] = val[b * steps:(b + 1) * steps].astype(ref.dtype)


def _halo_perm_matrices():
    prev_steps = CONV_W // 2
    halo_rows = SUBLANES * SUBLANES
    pp = np.zeros((prev_steps * SUBLANES, halo_rows), np.float32)
    pn = np.zeros((SUBLANES, halo_rows), np.float32)
    for b in range(SUBLANES):
        for j in range(prev_steps):
            pp[j * SUBLANES + b, b * SUBLANES + SUBLANES - prev_steps + j] = 1.0
        pn[b, b * SUBLANES] = 1.0
    return pp, pn


def _inproj_fwd_kernel(xprev_ref, x_ref, xnext_ref, g_ref, w_ref, b_ref, pp_ref, pn_ref,
                       cw_ref, cb_ref, wa_ref, ba_ref, wi_ref, bi_ref, lam_ref,
                       qkv_ref, gates_ref, rec_ref, carry, perm_s, wa_t, wi_t,
                       *, d_rec, d_model):
    t = pl.program_id(0)
    n_t = pl.num_programs(0)
    rows = SCAN_STEPS * SUBLANES
    prev_rows = (CONV_W // 2) * SUBLANES

    @pl.when(t == 0)
    def _():
        carry[...] = jnp.zeros_like(carry)
        _build_gate_tiles(wa_ref, wa_t)
        _build_gate_tiles(wi_ref, wi_t)

    def normed(ref):
        return _rms(ref[...].reshape(-1, d_model), g_ref[...])

    def permuted(p, hb):
        return jnp.dot(p[...], hb, preferred_element_type=F32).astype(BF16)

    def proj(lhs, lo, width):
        return jnp.dot(lhs, w_ref[:, lo:lo + width], preferred_element_type=F32) + b_ref[:, lo:lo + width]

    def step_major(hb):
        for c in range(d_model // LANES):
            for b in range(SUBLANES):
                perm_s[c, b * PERM_PITCH:b * PERM_PITCH + SCAN_STEPS, :] = (
                    hb[b * SCAN_STEPS:(b + 1) * SCAN_STEPS, c * LANES:(c + 1) * LANES])
        return jnp.concatenate(
            [jnp.concatenate([perm_s[c, pl.ds(ts, SUBLANES, stride=PERM_PITCH), :] for c in range(d_model // LANES)],
                             axis=1) for ts in range(SCAN_STEPS)], axis=0)

    h32 = normed(x_ref)
    h = h32.astype(BF16)
    hp = step_major(h32).astype(BF16)
    lhs_u = jnp.concatenate([permuted(pp_ref, normed(xprev_ref).astype(BF16)), hp,
                             permuted(pn_ref, normed(xnext_ref).astype(BF16))], axis=0)
    lo_u = 3 * D_ATT
    lo_y = lo_u + d_rec
    lo_ga = lo_y + d_rec

    def u_tile(kt):
        return proj(lhs_u, lo_u + kt * MXU_TILE, MXU_TILE)

    def rec_tile(kt, ucat):
        lo = kt * MXU_TILE
        rec_ref[:, lo:lo + MXU_TILE] = ucat[prev_rows:prev_rows + rows].astype(BF16)
        ucat = jnp.concatenate([jnp.where(t == 0, 0.0, ucat[:prev_rows]), ucat[prev_rows:prev_rows + rows],
                                jnp.where(t == n_t - 1, 0.0, ucat[prev_rows + rows:])], axis=0)
        h_fwd = _rglru_tile(ucat, kt, cw_ref, cb_ref, wa_t, ba_ref, wi_t, bi_ref, lam_ref, carry, reverse=False)
        rec_ref[:, 2 * d_rec + lo:2 * d_rec + lo + MXU_TILE] = h_fwd.astype(BF16)

    def att_tile(part, kt, scale):
        val = proj(h, part * D_ATT + kt * PROJ_TILE, PROJ_TILE)
        _store_batches(qkv_ref, val if scale is None else val * scale, 3 * D_ATT, part * D_ATT + kt * PROJ_TILE)

    def gy_tile(kt):
        lo = d_rec + kt * PROJ_TILE
        rec_ref[:, lo:lo + PROJ_TILE] = _gelu_tanh(proj(hp, lo_y + kt * PROJ_TILE, PROJ_TILE)).astype(BF16)

    def gate_tile(part, kt):
        val = _sigmoid(proj(h, lo_ga + part * d_model + kt * PROJ_TILE, PROJ_TILE))
        _store_batches(gates_ref, val, 2 * d_model, part * d_model + kt * PROJ_TILE)

    q_scale = HEAD_DIM ** -0.5 * LOG2E
    fillers = [functools.partial(att_tile, part, kt, q_scale if part == 0 else None)
               for part in range(3) for kt in range(D_ATT // PROJ_TILE)]
    fillers += [functools.partial(gy_tile, kt) for kt in range(d_rec // PROJ_TILE)]
    fillers += [functools.partial(gate_tile, part, kt) for part in range(2) for kt in range(d_model // PROJ_TILE)]
    n_rec = d_rec // MXU_TILE
    per_rec = -(-len(fillers) // n_rec)
    ucat_next = u_tile(0)
    for kt in range(n_rec):
        ucat = ucat_next
        if kt + 1 < n_rec:
            ucat_next = u_tile(kt + 1)
        for f in fillers[kt * per_rec:(kt + 1) * per_rec]:
            f()
        rec_tile(kt, ucat)


def _inproj_fwd(x, g, w, b, perms, rec_params, d_rec):
    bsz, s, d = x.shape
    rows = SCAN_STEPS * SUBLANES
    halo_blocks = SCAN_STEPS // SUBLANES
    n_halo = s // SUBLANES

    def tm(width):
        return jax.ShapeDtypeStruct((s, bsz * width), BF16), pl.BlockSpec((SCAN_STEPS, bsz * width), lambda t: (t, 0))

    def sm(width):
        return jax.ShapeDtypeStruct((s * bsz, width), BF16), pl.BlockSpec((rows, width), lambda t: (t, 0))

    outs = [tm(3 * D_ATT), tm(2 * d), sm(3 * d_rec)]
    consts = [g, w, b, *perms, *rec_params]
    return pl.pallas_call(
        functools.partial(_inproj_fwd_kernel, d_rec=d_rec, d_model=d),
        grid=(s // SCAN_STEPS,),
        in_specs=[
            pl.BlockSpec((bsz, SUBLANES, d), lambda t: (0, jnp.maximum(t * halo_blocks - 1, 0), 0)),
            pl.BlockSpec((bsz, SCAN_STEPS, d), lambda t: (0, t, 0)),
            pl.BlockSpec((bsz, SUBLANES, d), lambda t: (0, jnp.minimum((t + 1) * halo_blocks, n_halo - 1), 0)),
        ] + [_const_spec(c.shape) for c in consts],
        out_specs=[o[1] for o in outs],
        out_shape=[o[0] for o in outs],
        scratch_shapes=[pltpu.VMEM((SUBLANES, d_rec), F32),
                        pltpu.VMEM((d // LANES, SUBLANES * PERM_PITCH, LANES), F32),
                        _gate_tile_scratch(rec_params[2]), _gate_tile_scratch(rec_params[4])],
        compiler_params=pltpu.CompilerParams(dimension_semantics=("arbitrary",), vmem_limit_bytes=VMEM_LIMIT),
        name="inproj_rglru_fwd",
    )(x, x, x, *consts)


def _attn_kernel(q_ref, k0_ref, k1_ref, k2_ref, v0_ref, v1_ref, v2_ref, tlo_ref, thi_ref, o_ref, bias_ref,
                 *, n_rows):
    band = pl.program_id(0)
    rows_q = ATT_BAND * GRID_W
    lane_row = lax.broadcasted_iota(jnp.int32, (GRID_W, LANES), 1) < HEAD_DIM

    @pl.when(pl.program_id(1) == 0)
    def _():
        r0 = band * ATT_BAND
        k0 = jnp.clip(r0 - WIN_H // 2, 0, n_rows - ATT_KEY_ROWS)
        for i in range(ATT_BAND):
            rq = r0 + i
            rs = jnp.clip(rq - WIN_H // 2, 0, n_rows - WIN_H)
            for jp in range(ATT_KEY_ROWS // HEADS_PER_VREG):
                rka = k0 + 2 * jp
                rkb = rka + 1
                va = jnp.logical_and(rka >= rs, rka < rs + WIN_H)
                vb = jnp.logical_and(rkb >= rs, rkb < rs + WIN_H)
                dra = jnp.clip(rka - rq + (WIN_H - 1), 0, 2 * WIN_H - 2)
                drb = jnp.clip(rkb - rq + (WIN_H - 1), 0, 2 * WIN_H - 2)
                for h in range(N_HEADS):
                    a = jnp.where(va, tlo_ref[h, dra], NEG)
                    b = jnp.where(vb, thi_ref[h, drb], NEG)
                    bias_ref[h, i * GRID_W:(i + 1) * GRID_W, jp * LANES:(jp + 1) * LANES] = jnp.where(lane_row, a, b)

    first = lax.broadcasted_iota(jnp.int32, (rows_q, LANES), 1) < HEAD_DIM
    k_refs = (k0_ref, k1_ref, k2_ref)
    v_refs = (v0_ref, v1_ref, v2_ref)
    zero = jnp.zeros((), BF16)

    def scores(hd):
        j, half = divmod(hd, HEADS_PER_VREG)
        cs = slice(j * LANES, (j + 1) * LANES)
        qh = jnp.where(first if half == 0 else jnp.logical_not(first), q_ref[:, cs], zero)
        s = jnp.concatenate(
            [lax.dot_general(qh, r[:, cs], (((1,), (1,)), ((), ())), preferred_element_type=F32) for r in k_refs],
            axis=1) + bias_ref[hd]
        return s, jnp.max(s, axis=-1, keepdims=True)

    nxt = scores(0)
    outs = []
    for hd in range(N_HEADS):
        j = hd // HEADS_PER_VREG
        cs = slice(j * LANES, (j + 1) * LANES)
        s, m = nxt
        if hd + 1 < N_HEADS:
            nxt = scores(hd + 1)
        p = jnp.exp2(s - m)
        l = jnp.sum(p, axis=-1, keepdims=True)
        pb = p.astype(BF16)
        o = jnp.dot(pb[:, 0:MXU_TILE], v_refs[0][:, cs], preferred_element_type=F32)
        for jj in range(1, len(v_refs)):
            o = o + jnp.dot(pb[:, jj * MXU_TILE:(jj + 1) * MXU_TILE], v_refs[jj][:, cs], preferred_element_type=F32)
        outs.append(o / l)
        if len(outs) == HEADS_PER_VREG:
            o_ref[:, cs] = jnp.where(first, outs[0], outs[1]).astype(BF16)
            outs = []


def _attention(qkv, tlo, thi, bsz):
    s = qkv.shape[0]
    n_rows = s // GRID_W
    n_bands = n_rows // ATT_BAND
    rows_q = ATT_BAND * GRID_W
    n_kblk = ATT_KEY_ROWS * GRID_W // MXU_TILE
    assert rows_q == MXU_TILE and ATT_KEY_ROWS * GRID_W == n_kblk * MXU_TILE

    def kv_spec(part, jj):
        def idx(band, bi):
            return (jnp.clip(band - 1, 0, n_bands - n_kblk) + jj, 3 * bi + part)
        return pl.BlockSpec((MXU_TILE, D_ATT), idx)

    return pl.pallas_call(
        functools.partial(_attn_kernel, n_rows=n_rows),
        grid=(n_bands, bsz),
        in_specs=[pl.BlockSpec((rows_q, D_ATT), lambda band, bi: (band, 3 * bi))]
        + [kv_spec(1, jj) for jj in range(n_kblk)] + [kv_spec(2, jj) for jj in range(n_kblk)]
        + [_const_spec(tlo.shape), _const_spec(thi.shape)],
        out_specs=pl.BlockSpec((rows_q, D_ATT), lambda band, bi: (band, bi)),
        out_shape=jax.ShapeDtypeStruct((s, bsz * D_ATT), BF16),
        scratch_shapes=[pltpu.VMEM((N_HEADS, rows_q, ATT_KEY_ROWS * GRID_W), F32)],
        compiler_params=pltpu.CompilerParams(
            dimension_semantics=("arbitrary", "arbitrary"), vmem_limit_bytes=VMEM_LIMIT),
        name="nbr_attention",
    )(*([qkv] * (1 + 2 * n_kblk)), tlo, thi)


def _bias_tables(rpb):
    qc = np.arange(GRID_W)[:, None]
    kc = np.arange(GRID_W)[None, :]
    ws = np.clip(qc - WIN_W // 2, 0, GRID_W - WIN_W)
    valid = (kc >= ws) & (kc < ws + WIN_W)
    d_col = np.clip(kc - qc, -(WIN_W - 1), WIN_W - 1) + (WIN_W - 1)
    onehot = (np.arange(2 * WIN_W - 1)[:, None, None] == d_col[None]).astype(np.float32)
    t = jnp.einsum("hrd,dqk->hrqk", rpb.astype(F32), onehot, precision=lax.Precision.HIGHEST)
    t = jnp.where(valid[None, None], t * LOG2E, NEG)
    neg = jnp.full_like(t, NEG)
    return jnp.concatenate([t, neg], axis=-1), jnp.concatenate([neg, t], axis=-1)


def _rglru_tile(ucat, kt, cw_ref, cb_ref, wa_ref, ba_ref, wi_ref, bi_ref, lam_ref, carry, reverse):
    rows = SCAN_STEPS * SUBLANES
    cs = slice(kt * MXU_TILE, (kt + 1) * MXU_TILE)
    conv = cb_ref[:, cs]
    for j in range(CONV_W):
        conv = conv + ucat[j * SUBLANES:j * SUBLANES + rows] * cw_ref[j:j + 1, cs]
    conv_b = conv.astype(BF16)

    def gate(w_ref, b_ref):
        return _sigmoid(jnp.dot(conv_b, w_ref[kt], preferred_element_type=F32) + b_ref[:, cs])

    r_gate = gate(wa_ref, ba_ref)
    i_gate = gate(wi_ref, bi_ref)
    log_a = r_gate * (-LRU_C * _softplus(-lam_ref[:, cs]))
    a = jnp.exp(log_a)
    z = jnp.maximum(jnp.tanh(log_a) * (-1.0 - a * a), 0.0)
    mult = jnp.where(z > 0.0, z * lax.rsqrt(z), 0.0)
    bx = mult * (i_gate * conv)

    h = carry[:, cs]
    hs = [None] * SCAN_STEPS
    for i in range(SCAN_STEPS):
        t = SCAN_STEPS - 1 - i if reverse else i
        sl = slice(t * SUBLANES, (t + 1) * SUBLANES)
        h = a[sl] * h + bx[sl]
        hs[t] = h
    carry[:, cs] = h
    return jnp.concatenate(hs, axis=0)


def _tail_kernel(uprev_ref, rec_ref, unext_ref, cw_ref, cb_ref, wa_ref, ba_ref, wi_ref, bi_ref, lam_ref,
                 x_ref, att_ref, gates_ref, wao_ref, wro_ref, wout_ref,
                 g2_ref, w1_ref, w2_ref, gf_ref, o_ref, carry, y_s, wa_t, wi_t, *, ff_chunk):
    j = pl.program_id(0)
    n_t = pl.num_programs(0) - 1
    rows = SCAN_STEPS * SUBLANES
    d = x_ref.shape[-1]
    prev_rows = (CONV_W // 2) * SUBLANES
    t_idx = n_t - 1 - jnp.minimum(j, n_t - 1)
    n_tiles = d // MXU_TILE

    def rec_tile(kt):
        cs = slice(kt * MXU_TILE, (kt + 1) * MXU_TILE)
        gy_cs = slice(d + kt * MXU_TILE, d + (kt + 1) * MXU_TILE)
        hf_cs = slice(2 * d + kt * MXU_TILE, 2 * d + (kt + 1) * MXU_TILE)
        prev = jnp.where(t_idx == 0, 0.0, uprev_ref[:, cs].astype(F32)[BF16_ROWS - prev_rows:])
        after = jnp.where(t_idx == n_t - 1, 0.0, unext_ref[:, cs].astype(F32)[0:SUBLANES])
        ucat = jnp.concatenate([prev, rec_ref[:, cs].astype(F32), after], axis=0)
        h_bwd = _rglru_tile(ucat, kt, cw_ref, cb_ref, wa_t, ba_ref, wi_t, bi_ref, lam_ref, carry, reverse=True)
        y = (rec_ref[:, hf_cs].astype(F32) + h_bwd) * rec_ref[:, gy_cs].astype(F32)
        for ts in range(SCAN_STEPS):
            for ci in range(MXU_TILE // LANES):
                y_s[kt * (MXU_TILE // LANES) + ci, pl.ds(ts, SUBLANES, stride=PERM_PITCH), :] = (
                    y[ts * SUBLANES:(ts + 1) * SUBLANES, ci * LANES:(ci + 1) * LANES])

    @pl.when(j == 0)
    def _():
        carry[...] = jnp.zeros_like(carry)
        _build_gate_tiles(wa_ref, wa_t)
        _build_gate_tiles(wi_ref, wi_t)
        for kt in range(n_tiles):
            rec_tile(kt)

    @pl.when(j > 0)
    def _():
        y_nat = jnp.concatenate(
            [jnp.concatenate([y_s[c, b * PERM_PITCH:b * PERM_PITCH + SCAN_STEPS, :] for c in range(d // LANES)],
                             axis=1) for b in range(SUBLANES)], axis=0).astype(BF16)
        y_rec = jnp.dot(y_nat, wro_ref[...], preferred_element_type=F32)
        y_att = jnp.dot(_stack_batches(att_ref, D_ATT), wao_ref[...], preferred_element_type=F32)
        mixed = (_stack_batches(gates_ref, 2 * d, 0, d).astype(F32) * y_att
                 + _stack_batches(gates_ref, 2 * d, d, d).astype(F32) * y_rec)
        out = x_ref[...].reshape(rows, d) + jnp.dot(mixed.astype(BF16), wout_ref[...], preferred_element_type=F32)

        h2 = _rms(out, g2_ref[...]).astype(BF16)
        d_ff = w1_ref.shape[1]
        n_chunks = d_ff // ff_chunk
        for c in range(max(n_chunks, n_tiles)):
            if c < n_chunks:
                fs = slice(c * ff_chunk, (c + 1) * ff_chunk)
                hid = jnp.maximum(jnp.dot(h2, w1_ref[:, fs], preferred_element_type=F32), 0.0)
                out = out + jnp.dot((hid * hid).astype(BF16), w2_ref[fs, :], preferred_element_type=F32)
            if c < n_tiles:
                rec_tile(c)
        o_ref[...] = _rms(out, gf_ref[...]).reshape(o_ref.shape)


def _tail(rec, params, x, att, gates, wao, wro, wout, g2, w1, w2, gf):
    n_rows = rec.shape[0]
    c = rec.shape[1] // 3
    bsz, s, d = x.shape
    rows = SCAN_STEPS * SUBLANES
    n_t = s // SCAN_STEPS
    halo_per_block = rows // BF16_ROWS
    n_halo = n_rows // BF16_ROWS

    def scan_blk(j):
        return n_t - 1 - jnp.minimum(j, n_t - 1)

    def merge_blk(j):
        return n_t - 1 - jnp.maximum(j - 1, 0)

    halo_specs = [
        pl.BlockSpec((BF16_ROWS, c), lambda j: (jnp.maximum(scan_blk(j) * halo_per_block - 1, 0), 0)),
        pl.BlockSpec((rows, 3 * c), lambda j: (scan_blk(j), 0)),
        pl.BlockSpec((BF16_ROWS, c), lambda j: (jnp.minimum((scan_blk(j) + 1) * halo_per_block, n_halo - 1), 0)),
    ]

    def tm(width):
        return pl.BlockSpec((SCAN_STEPS, bsz * width), lambda j: (merge_blk(j), 0))

    x_spec = pl.BlockSpec((bsz, SCAN_STEPS, d), lambda j: (0, merge_blk(j), 0))
    weights = [wao, wro, wout, g2, w1, w2, gf]
    return pl.pallas_call(
        functools.partial(_tail_kernel, ff_chunk=FF_CHUNK),
        grid=(n_t + 1,),
        in_specs=halo_specs + [_const_spec(p.shape) for p in params]
        + [x_spec, tm(D_ATT), tm(2 * d)]
        + [_const_spec(p.shape) for p in weights],
        out_specs=x_spec,
        out_shape=jax.ShapeDtypeStruct(x.shape, F32),
        scratch_shapes=[pltpu.VMEM((SUBLANES, c), F32), pltpu.VMEM((c // LANES, SUBLANES * PERM_PITCH, LANES), F32),
                        _gate_tile_scratch(params[2]), _gate_tile_scratch(params[4])],
        compiler_params=pltpu.CompilerParams(dimension_semantics=("arbitrary",), vmem_limit_bytes=VMEM_LIMIT_TAIL),
        name="rglru_bwd_merge_ffn",
    )(rec, rec, rec, *params, x, att, gates, *weights)


def _build_gate_tiles(w_ref, tile_ref):
    n_blocks, bw, _ = w_ref.shape
    per = MXU_TILE // bw
    tile_ref[...] = jnp.zeros_like(tile_ref)
    for n in range(n_blocks):
        kt, p = divmod(n, per)
        tile_ref[kt, p * bw:(p + 1) * bw, p * bw:(p + 1) * bw] = w_ref[n].astype(tile_ref.dtype)


def _gate_tile_scratch(w):
    n_blocks, bw, _ = w.shape
    return pltpu.VMEM((n_blocks * bw // MXU_TILE, MXU_TILE, MXU_TILE), BF16)


def kernel(x, ln1_g, w_in, b_in, rpb, w_att_o, conv_w, conv_b, w_rg_a, b_rg_a, w_rg_i, b_rg_i, lru_lambda,
           w_rec_o, w_out, ln2_g, w_ff1, w_ff2, lnf_g):
    bsz, s, d = x.shape
    d_rec = conv_w.shape[-1]
    assert ln1_g.shape[0] == 1, "single-layer stack only"
    assert bsz == SUBLANES and s % SCAN_STEPS == 0 and s % (ATT_BAND * GRID_W) == 0 and w_ff1.shape[-1] % FF_CHUNK == 0
    assert w_in.shape[-1] == 3 * D_ATT + 2 * d_rec + 2 * d and d_rec % MXU_TILE == 0 and d_rec == d
    row = lambda p: p.reshape(1, -1).astype(F32)
    l = 0

    def dir_params(di):
        return [conv_w[l].astype(F32), row(conv_b[l]), w_rg_a[l, di].astype(F32), row(b_rg_a[l, di]),
                w_rg_i[l, di].astype(F32), row(b_rg_i[l, di]), row(lru_lambda[l, di])]

    perms = [jnp.asarray(m, BF16) for m in _halo_perm_matrices()]
    qkv, gates, rec = _inproj_fwd(x, row(ln1_g[l]), w_in[l].astype(BF16), row(b_in[l]), perms, dir_params(0), d_rec)

    tlo, thi = _bias_tables(rpb[l])
    att = _attention(qkv, tlo, thi, bsz)

    return _tail(rec, dir_params(1), x, att, gates,
                 w_att_o[l].astype(BF16), w_rec_o[l].astype(BF16), w_out[l].astype(BF16),
                 row(ln2_g[l]), w_ff1[l].astype(BF16), w_ff2[l].astype(BF16), row(lnf_g))
```

```python
import functools

import jax
import jax.numpy as jnp
import numpy as np
from jax import lax
from jax.experimental import pallas as pl
from jax.experimental.pallas import tpu as pltpu

F32 = jnp.float32
BF16 = jnp.bfloat16

GRID_W = 64
N_HEADS = 8
HEAD_DIM = 64
D_ATT = N_HEADS * HEAD_DIM
WIN_H = 8
WIN_W = 16
CONV_W = 4
LRU_C = 8.0
EPS = 1e-6
NEG = -1e30
LOG2E = float(np.log2(np.e))

LANES = 128
SUBLANES = 8
BF16_ROWS = 16
MXU_TILE = 256
VMEM_LIMIT = 56 * 1024 * 1024
VMEM_LIMIT_TAIL = 60 * 1024 * 1024

PROJ_TILE = 256
FF_CHUNK = 1024
SCAN_STEPS = 64
PERM_PITCH = SCAN_STEPS + SUBLANES
ATT_BAND = 4
ATT_KEY_ROWS = 12
HEADS_PER_VREG = LANES // HEAD_DIM


def _rms(x, g):
    ms = jnp.mean(x * x, axis=-1, keepdims=True)
    return x * lax.rsqrt(ms + EPS) * g


def _sigmoid(x):
    return 1.0 / (1.0 + jnp.exp2(x * (-LOG2E)))


def _gelu_tanh(x):
    c = float(np.sqrt(2.0 / np.pi))
    return 0.5 * x * (1.0 + jnp.tanh(c * (x + 0.044715 * (x * x * x))))


def _softplus(x):
    return jnp.maximum(x, 0.0) + jnp.log1p(jnp.exp(-jnp.abs(x)))


def _const_spec(shape):
    nd = len(shape)
    return pl.BlockSpec(shape, lambda *_: (0,) * nd, pipeline_mode=pl.Buffered(1))


def _stack_batches(ref, c, lo=0, width=None):
    width = c if width is None else width
    return jnp.concatenate([ref[:, b * c + lo:b * c + lo + width] for b in range(SUBLANES)], axis=0)


def _store_batches(ref, val, c, lo=0):
    steps = ref.shape[0]
    width = val.shape[1]
    for b in range(SUBLANES):
        ref[:, b * c + lo:b * c + lo + width] = val[b * steps:(b + 1) * steps].astype(ref.dtype)


def _halo_perm_matrices():
    prev_steps = CONV_W // 2
    halo_rows = SUBLANES * SUBLANES
    pp = np.zeros((prev_steps * SUBLANES, halo_rows), np.float32)
    pn = np.zeros((SUBLANES, halo_rows), np.float32)
    for b in range(SUBLANES):
        for j in range(prev_steps):
            pp[j * SUBLANES + b, b * SUBLANES + SUBLANES - prev_steps + j] = 1.0
        pn[b, b * SUBLANES] = 1.0
    return pp, pn


def _inproj_fwd_kernel(xprev_ref, x_ref, xnext_ref, g_ref, w_ref, b_ref, pp_ref, pn_ref,
                       cw_ref, cb_ref, wa_ref, ba_ref, wi_ref, bi_ref, lam_ref,
                       q_ref, k_ref, v_ref, sga_ref, sgr_ref, u_ref, gy_ref, hf_ref, carry, perm_s, wa_t, wi_t,
                       *, d_rec, d_model):
    t = pl.program_id(0)
    n_t = pl.num_programs(0)
    rows = SCAN_STEPS * SUBLANES
    prev_rows = (CONV_W // 2) * SUBLANES

    @pl.when(t == 0)
    def _():
        carry[...] = jnp.zeros_like(carry)
        _build_gate_tiles(wa_ref, wa_t)
        _build_gate_tiles(wi_ref, wi_t)

    def normed(ref):
        return _rms(ref[...].reshape(-1, d_model), g_ref[...])

    def permuted(p, hb):
        return jnp.dot(p[...], hb, preferred_element_type=F32).astype(BF16)

    def proj(lhs, lo, width):
        return jnp.dot(lhs, w_ref[:, lo:lo + width], preferred_element_type=F32) + b_ref[:, lo:lo + width]

    def step_major(hb):
        for c in range(d_model // LANES):
            for b in range(SUBLANES):
                for g in range(SCAN_STEPS // SUBLANES):
                    src = b * SCAN_STEPS + g * SUBLANES
                    perm_s[c, pl.ds(g * SUBLANES * SUBLANES + b, SUBLANES, stride=SUBLANES), :] = (
                        hb[src:src + SUBLANES, c * LANES:(c + 1) * LANES])
        return jnp.concatenate([perm_s[c] for c in range(d_model // LANES)], axis=1)

    h32 = normed(x_ref)
    h = h32.astype(BF16)
    hp = step_major(h32).astype(BF16)
    lhs_u = jnp.concatenate([permuted(pp_ref, normed(xprev_ref).astype(BF16)), hp,
                             permuted(pn_ref, normed(xnext_ref).astype(BF16))], axis=0)
    lo_u = 3 * D_ATT
    lo_y = lo_u + d_rec
    lo_ga = lo_y + d_rec
    lo_gr = lo_ga + d_model

    def u_tile(kt):
        return proj(lhs_u, lo_u + kt * MXU_TILE, MXU_TILE)

    def rec_tile(kt, ucat):
        cs = slice(kt * MXU_TILE, (kt + 1) * MXU_TILE)
        u_ref[:, cs] = ucat[prev_rows:prev_rows + rows].astype(BF16)
        ucat = jnp.concatenate([jnp.where(t == 0, 0.0, ucat[:prev_rows]), ucat[prev_rows:prev_rows + rows],
                                jnp.where(t == n_t - 1, 0.0, ucat[prev_rows + rows:])], axis=0)
        h_fwd = _rglru_tile(ucat, kt, cw_ref, cb_ref, wa_t, ba_ref, wi_t, bi_ref, lam_ref, carry, reverse=False)
        hf_ref[:, cs] = h_fwd.astype(BF16)

    def att_tile(ref, lo, kt, scale):
        val = proj(h, lo + kt * PROJ_TILE, PROJ_TILE)
        _store_batches(ref, val if scale is None else val * scale, D_ATT, kt * PROJ_TILE)

    def gy_tile(kt):
        cs = slice(kt * PROJ_TILE, (kt + 1) * PROJ_TILE)
        gy_ref[:, cs] = _gelu_tanh(proj(hp, lo_y + kt * PROJ_TILE, PROJ_TILE)).astype(BF16)

    def gate_tile(ref, lo, kt):
        _store_batches(ref, _sigmoid(proj(h, lo + kt * PROJ_TILE, PROJ_TILE)), d_model, kt * PROJ_TILE)

    fillers = [functools.partial(att_tile, q_ref, 0, kt, HEAD_DIM ** -0.5 * LOG2E) for kt in range(D_ATT // PROJ_TILE)]
    fillers += [functools.partial(att_tile, k_ref, D_ATT, kt, None) for kt in range(D_ATT // PROJ_TILE)]
    fillers += [functools.partial(att_tile, v_ref, 2 * D_ATT, kt, None) for kt in range(D_ATT // PROJ_TILE)]
    fillers += [functools.partial(gy_tile, kt) for kt in range(d_rec // PROJ_TILE)]
    fillers += [functools.partial(gate_tile, sga_ref, lo_ga, kt) for kt in range(d_model // PROJ_TILE)]
    fillers += [functools.partial(gate_tile, sgr_ref, lo_gr, kt) for kt in range(d_model // PROJ_TILE)]
    n_rec = d_rec // MXU_TILE
    per_rec = -(-len(fillers) // n_rec)
    ucat_next = u_tile(0)
    for kt in range(n_rec):
        ucat = ucat_next
        if kt + 1 < n_rec:
            ucat_next = u_tile(kt + 1)
        for f in fillers[kt * per_rec:(kt + 1) * per_rec]:
            f()
        rec_tile(kt, ucat)


def _inproj_fwd(x, g, w, b, perms, rec_params, d_rec):
    bsz, s, d = x.shape
    rows = SCAN_STEPS * SUBLANES
    halo_blocks = SCAN_STEPS // SUBLANES
    n_halo = s // SUBLANES

    def tm(width):
        return jax.ShapeDtypeStruct((s, bsz * width), BF16), pl.BlockSpec((SCAN_STEPS, bsz * width), lambda t: (t, 0))

    def sm(width):
        return jax.ShapeDtypeStruct((s * bsz, width), BF16), pl.BlockSpec((rows, width), lambda t: (t, 0))

    outs = [tm(D_ATT), tm(D_ATT), tm(D_ATT), tm(d), tm(d), sm(d_rec), sm(d_rec), sm(d_rec)]
    consts = [g, w, b, *perms, *rec_params]
    return pl.pallas_call(
        functools.partial(_inproj_fwd_kernel, d_rec=d_rec, d_model=d),
        grid=(s // SCAN_STEPS,),
        in_specs=[
            pl.BlockSpec((bsz, SUBLANES, d), lambda t: (0, jnp.maximum(t * halo_blocks - 1, 0), 0)),
            pl.BlockSpec((bsz, SCAN_STEPS, d), lambda t: (0, t, 0)),
            pl.BlockSpec((bsz, SUBLANES, d), lambda t: (0, jnp.minimum((t + 1) * halo_blocks, n_halo - 1), 0)),
        ] + [_const_spec(c.shape) for c in consts],
        out_specs=[o[1] for o in outs],
        out_shape=[o[0] for o in outs],
        scratch_shapes=[pltpu.VMEM((SUBLANES, d_rec), F32),
                        pltpu.VMEM((d // LANES, rows, LANES), F32),
                        _gate_tile_scratch(rec_params[2]), _gate_tile_scratch(rec_params[4])],
        compiler_params=pltpu.CompilerParams(dimension_semantics=("arbitrary",), vmem_limit_bytes=VMEM_LIMIT),
        name="inproj_rglru_fwd",
    )(x, x, x, *consts)


def _attn_kernel(q_ref, k0_ref, k1_ref, k2_ref, v0_ref, v1_ref, v2_ref, tlo_ref, thi_ref, o_ref, bias_ref,
                 *, n_rows):
    band = pl.program_id(0)
    rows_q = ATT_BAND * GRID_W
    lane_row = lax.broadcasted_iota(jnp.int32, (GRID_W, LANES), 1) < HEAD_DIM

    @pl.when(pl.program_id(1) == 0)
    def _():
        r0 = band * ATT_BAND
        k0 = jnp.clip(r0 - WIN_H // 2, 0, n_rows - ATT_KEY_ROWS)
        for i in range(ATT_BAND):
            rq = r0 + i
            rs = jnp.clip(rq - WIN_H // 2, 0, n_rows - WIN_H)
            for jp in range(ATT_KEY_ROWS // HEADS_PER_VREG):
                rka = k0 + 2 * jp
                rkb = rka + 1
                va = jnp.logical_and(rka >= rs, rka < rs + WIN_H)
                vb = jnp.logical_and(rkb >= rs, rkb < rs + WIN_H)
                dra = jnp.clip(rka - rq + (WIN_H - 1), 0, 2 * WIN_H - 2)
                drb = jnp.clip(rkb - rq + (WIN_H - 1), 0, 2 * WIN_H - 2)
                for h in range(N_HEADS):
                    a = jnp.where(va, tlo_ref[h, dra], NEG)
                    b = jnp.where(vb, thi_ref[h, drb], NEG)
                    bias_ref[h, i * GRID_W:(i + 1) * GRID_W, jp * LANES:(jp + 1) * LANES] = jnp.where(lane_row, a, b)

    first = lax.broadcasted_iota(jnp.int32, (rows_q, LANES), 1) < HEAD_DIM
    k_refs = (k0_ref, k1_ref, k2_ref)
    v_refs = (v0_ref, v1_ref, v2_ref)
    zero = jnp.zeros((), BF16)

    def scores(hd):
        j, half = divmod(hd, HEADS_PER_VREG)
        cs = slice(j * LANES, (j + 1) * LANES)
        qh = jnp.where(first if half == 0 else jnp.logical_not(first), q_ref[:, cs], zero)
        s = jnp.concatenate(
            [lax.dot_general(qh, r[:, cs], (((1,), (1,)), ((), ())), preferred_element_type=F32) for r in k_refs],
            axis=1) + bias_ref[hd]
        return s, jnp.max(s, axis=-1, keepdims=True)

    nxt = scores(0)
    outs = []
    for hd in range(N_HEADS):
        j = hd // HEADS_PER_VREG
        cs = slice(j * LANES, (j + 1) * LANES)
        s, m = nxt
        if hd + 1 < N_HEADS:
            nxt = scores(hd + 1)
        p = jnp.exp2(s - m)
        l = jnp.sum(p, axis=-1, keepdims=True)
        pb = p.astype(BF16)
        o = jnp.dot(pb[:, 0:MXU_TILE], v_refs[0][:, cs], preferred_element_type=F32)
        for jj in range(1, len(v_refs)):
            o = o + jnp.dot(pb[:, jj * MXU_TILE:(jj + 1) * MXU_TILE], v_refs[jj][:, cs], preferred_element_type=F32)
        outs.append(o / l)
        if len(outs) == HEADS_PER_VREG:
            o_ref[:, cs] = jnp.where(first, outs[0], outs[1]).astype(BF16)
            outs = []


def _attention(q, k, v, tlo, thi, bsz):
    s = q.shape[0]
    n_rows = s // GRID_W
    n_bands = n_rows // ATT_BAND
    rows_q = ATT_BAND * GRID_W
    n_kblk = ATT_KEY_ROWS * GRID_W // MXU_TILE
    assert rows_q == MXU_TILE and ATT_KEY_ROWS * GRID_W == n_kblk * MXU_TILE

    def kv_spec(jj):
        def idx(band, bi):
            return (jnp.clip(band - 1, 0, n_bands - n_kblk) + jj, bi)
        return pl.BlockSpec((MXU_TILE, D_ATT), idx)

    return pl.pallas_call(
        functools.partial(_attn_kernel, n_rows=n_rows),
        grid=(n_bands, bsz),
        in_specs=[pl.BlockSpec((rows_q, D_ATT), lambda band, bi: (band, bi))]
        + [kv_spec(jj) for jj in range(n_kblk)] * 2
        + [_const_spec(tlo.shape), _const_spec(thi.shape)],
        out_specs=pl.BlockSpec((rows_q, D_ATT), lambda band, bi: (band, bi)),
        out_shape=jax.ShapeDtypeStruct((s, bsz * D_ATT), BF16),
        scratch_shapes=[pltpu.VMEM((N_HEADS, rows_q, ATT_KEY_ROWS * GRID_W), F32)],
        compiler_params=pltpu.CompilerParams(
            dimension_semantics=("arbitrary", "arbitrary"), vmem_limit_bytes=VMEM_LIMIT),
        name="nbr_attention",
    )(q, k, k, k, v, v, v, tlo, thi)


def _bias_tables(rpb):
    qc = np.arange(GRID_W)[:, None]
    kc = np.arange(GRID_W)[None, :]
    ws = np.clip(qc - WIN_W // 2, 0, GRID_W - WIN_W)
    valid = (kc >= ws) & (kc < ws + WIN_W)
    d_col = np.clip(kc - qc, -(WIN_W - 1), WIN_W - 1) + (WIN_W - 1)
    onehot = (np.arange(2 * WIN_W - 1)[:, None, None] == d_col[None]).astype(np.float32)
    t = jnp.einsum("hrd,dqk->hrqk", rpb.astype(F32), onehot, precision=lax.Precision.HIGHEST)
    t = jnp.where(valid[None, None], t * LOG2E, NEG)
    neg = jnp.full_like(t, NEG)
    return jnp.concatenate([t, neg], axis=-1), jnp.concatenate([neg, t], axis=-1)


def _rglru_tile(ucat, kt, cw_ref, cb_ref, wa_ref, ba_ref, wi_ref, bi_ref, lam_ref, carry, reverse):
    rows = SCAN_STEPS * SUBLANES
    cs = slice(kt * MXU_TILE, (kt + 1) * MXU_TILE)
    conv = cb_ref[:, cs]
    for j in range(CONV_W):
        conv = conv + ucat[j * SUBLANES:j * SUBLANES + rows] * cw_ref[j:j + 1, cs]
    conv_b = conv.astype(BF16)

    def gate(w_ref, b_ref):
        return _sigmoid(jnp.dot(conv_b, w_ref[kt], preferred_element_type=F32) + b_ref[:, cs])

    r_gate = gate(wa_ref, ba_ref)
    i_gate = gate(wi_ref, bi_ref)
    log_a = r_gate * (-LRU_C * _softplus(-lam_ref[:, cs]))
    a = jnp.exp(log_a)
    z = jnp.maximum(jnp.tanh(log_a) * (-1.0 - a * a), 0.0)
    mult = jnp.where(z > 0.0, z * lax.rsqrt(z), 0.0)
    bx = mult * (i_gate * conv)

    h = carry[:, cs]
    hs = [None] * SCAN_STEPS
    for i in range(SCAN_STEPS):
        t = SCAN_STEPS - 1 - i if reverse else i
        sl = slice(t * SUBLANES, (t + 1) * SUBLANES)
        h = a[sl] * h + bx[sl]
        hs[t] = h
    carry[:, cs] = h
    return jnp.concatenate(hs, axis=0)


def _tail_kernel(uprev_ref, u_ref, unext_ref, cw_ref, cb_ref, wa_ref, ba_ref, wi_ref, bi_ref, lam_ref,
                 gy_ref, hf_ref, x_ref, att_ref, sga_ref, sgr_ref, wao_ref, wro_ref, wout_ref,
                 g2_ref, w1_ref, w2_ref, gf_ref, o_ref, carry, y_s, wa_t, wi_t, *, ff_chunk):
    j = pl.program_id(0)
    n_t = pl.num_programs(0) - 1
    rows = SCAN_STEPS * SUBLANES
    d = x_ref.shape[-1]
    prev_rows = (CONV_W // 2) * SUBLANES
    t_idx = n_t - 1 - jnp.minimum(j, n_t - 1)
    n_tiles = d // MXU_TILE

    def rec_tile(kt):
        cs = slice(kt * MXU_TILE, (kt + 1) * MXU_TILE)
        prev = jnp.where(t_idx == 0, 0.0, uprev_ref[:, cs].astype(F32)[BF16_ROWS - prev_rows:])
        after = jnp.where(t_idx == n_t - 1, 0.0, unext_ref[:, cs].astype(F32)[0:SUBLANES])
        ucat = jnp.concatenate([prev, u_ref[:, cs].astype(F32), after], axis=0)
        h_bwd = _rglru_tile(ucat, kt, cw_ref, cb_ref, wa_t, ba_ref, wi_t, bi_ref, lam_ref, carry, reverse=True)
        y = (hf_ref[:, cs].astype(F32) + h_bwd) * gy_ref[:, cs].astype(F32)
        for ts in range(SCAN_STEPS):
            for ci in range(MXU_TILE // LANES):
                y_s[kt * (MXU_TILE // LANES) + ci, pl.ds(ts, SUBLANES, stride=PERM_PITCH), :] = (
                    y[ts * SUBLANES:(ts + 1) * SUBLANES, ci * LANES:(ci + 1) * LANES])

    @pl.when(j == 0)
    def _():
        carry[...] = jnp.zeros_like(carry)
        _build_gate_tiles(wa_ref, wa_t)
        _build_gate_tiles(wi_ref, wi_t)
        for kt in range(n_tiles):
            rec_tile(kt)

    @pl.when(j > 0)
    def _():
        y_nat = jnp.concatenate(
            [jnp.concatenate([y_s[c, b * PERM_PITCH:b * PERM_PITCH + SCAN_STEPS, :] for c in range(d // LANES)],
                             axis=1) for b in range(SUBLANES)], axis=0).astype(BF16)
        y_rec = jnp.dot(y_nat, wro_ref[...], preferred_element_type=F32)
        y_att = jnp.dot(_stack_batches(att_ref, D_ATT), wao_ref[...], preferred_element_type=F32)
        mixed = _stack_batches(sga_ref, d).astype(F32) * y_att + _stack_batches(sgr_ref, d).astype(F32) * y_rec
        out = x_ref[...].reshape(rows, d) + jnp.dot(mixed.astype(BF16), wout_ref[...], preferred_element_type=F32)

        h2 = _rms(out, g2_ref[...]).astype(BF16)
        d_ff = w1_ref.shape[1]
        n_chunks = d_ff // ff_chunk
        for c in range(max(n_chunks, n_tiles)):
            if c < n_chunks:
                fs = slice(c * ff_chunk, (c + 1) * ff_chunk)
                hid = jnp.maximum(jnp.dot(h2, w1_ref[:, fs], preferred_element_type=F32), 0.0)
                out = out + jnp.dot((hid * hid).astype(BF16), w2_ref[fs, :], preferred_element_type=F32)
            if c < n_tiles:
                rec_tile(c)
        o_ref[...] = _rms(out, gf_ref[...]).reshape(o_ref.shape)


def _tail(u, params, gy, h_fwd, x, att, sga, sgr, wao, wro, wout, g2, w1, w2, gf):
    n_rows, c = u.shape
    bsz, s, d = x.shape
    rows = SCAN_STEPS * SUBLANES
    n_t = s // SCAN_STEPS
    halo_per_block = rows // BF16_ROWS
    n_halo = n_rows // BF16_ROWS

    def scan_blk(j):
        return n_t - 1 - jnp.minimum(j, n_t - 1)

    def merge_blk(j):
        return n_t - 1 - jnp.maximum(j - 1, 0)

    sm_spec = pl.BlockSpec((rows, c), lambda j: (scan_blk(j), 0))
    halo_specs = [
        pl.BlockSpec((BF16_ROWS, c), lambda j: (jnp.maximum(scan_blk(j) * halo_per_block - 1, 0), 0)),
        sm_spec,
        pl.BlockSpec((BF16_ROWS, c), lambda j: (jnp.minimum((scan_blk(j) + 1) * halo_per_block, n_halo - 1), 0)),
    ]

    def tm(width):
        return pl.BlockSpec((SCAN_STEPS, bsz * width), lambda j: (merge_blk(j), 0))

    x_spec = pl.BlockSpec((bsz, SCAN_STEPS, d), lambda j: (0, merge_blk(j), 0))
    weights = [wao, wro, wout, g2, w1, w2, gf]
    return pl.pallas_call(
        functools.partial(_tail_kernel, ff_chunk=FF_CHUNK),
        grid=(n_t + 1,),
        in_specs=halo_specs + [_const_spec(p.shape) for p in params] + [sm_spec, sm_spec]
        + [x_spec, tm(D_ATT), tm(d), tm(d)]
        + [_const_spec(p.shape) for p in weights],
        out_specs=x_spec,
        out_shape=jax.ShapeDtypeStruct(x.shape, F32),
        scratch_shapes=[pltpu.VMEM((SUBLANES, c), F32), pltpu.VMEM((c // LANES, SUBLANES * PERM_PITCH, LANES), F32),
                        _gate_tile_scratch(params[2]), _gate_tile_scratch(params[4])],
        compiler_params=pltpu.CompilerParams(dimension_semantics=("arbitrary",), vmem_limit_bytes=VMEM_LIMIT_TAIL),
        name="rglru_bwd_merge_ffn",
    )(u, u, u, *params, gy, h_fwd, x, att, sga, sgr, *weights)


def _build_gate_tiles(w_ref, tile_ref):
    n_blocks, bw, _ = w_ref.shape
    per = MXU_TILE // bw
    tile_ref[...] = jnp.zeros_like(tile_ref)
    for n in range(n_blocks):
        kt, p = divmod(n, per)
        tile_ref[kt, p * bw:(p + 1) * bw, p * bw:(p + 1) * bw] = w_ref[n].astype(tile_ref.dtype)


def _gate_tile_scratch(w):
    n_blocks, bw, _ = w.shape
    return pltpu.VMEM((n_blocks * bw // MXU_TILE, MXU_TILE, MXU_TILE), BF16)


def kernel(x, ln1_g, w_in, b_in, rpb, w_att_o, conv_w, conv_b, w_rg_a, b_rg_a, w_rg_i, b_rg_i, lru_lambda,
           w_rec_o, w_out, ln2_g, w_ff1, w_ff2, lnf_g):
    bsz, s, d = x.shape
    d_rec = conv_w.shape[-1]
    assert ln1_g.shape[0] == 1, "single-layer stack only"
    assert bsz == SUBLANES and s % SCAN_STEPS == 0 and s % (ATT_BAND * GRID_W) == 0 and w_ff1.shape[-1] % FF_CHUNK == 0
    assert w_in.shape[-1] == 3 * D_ATT + 2 * d_rec + 2 * d and d_rec % MXU_TILE == 0 and d_rec == d
    row = lambda p: p.reshape(1, -1).astype(F32)
    l = 0

    def dir_params(di):
        return [conv_w[l].astype(F32), row(conv_b[l]), w_rg_a[l, di].astype(F32), row(b_rg_a[l, di]),
                w_rg_i[l, di].astype(F32), row(b_rg_i[l, di]), row(lru_lambda[l, di])]

    perms = [jnp.asarray(m, BF16) for m in _halo_perm_matrices()]
    q, k, v, sga, sgr, u, gy, h_fwd = _inproj_fwd(x, row(ln1_g[l]), w_in[l].astype(BF16), row(b_in[l]), perms,
                                                  dir_params(0), d_rec)

    tlo, thi = _bias_tables(rpb[l])
    att = _attention(q, k, v, tlo, thi, bsz)

    return _tail(u, dir_params(1), gy, h_fwd, x, att, sga, sgr,
                 w_att_o[l].astype(BF16), w_rec_o[l].astype(BF16), w_out[l].astype(BF16),
                 row(ln2_g[l]), w_ff1[l].astype(BF16), w_ff2[l].astype(BF16), row(lnf_g))
```

```python
import functools

import jax
import jax.numpy as jnp
import numpy as np
from jax import lax
from jax.experimental import pallas as pl
from jax.experimental.pallas import tpu as pltpu

F32 = jnp.float32
BF16 = jnp.bfloat16

GRID_W = 64
N_HEADS = 8
HEAD_DIM = 64
D_ATT = N_HEADS * HEAD_DIM
WIN_H = 8
WIN_W = 16
CONV_W = 4
LRU_C = 8.0
EPS = 1e-6
NEG = -1e30
LOG2E = float(np.log2(np.e))

LANES = 128
SUBLANES = 8
BF16_ROWS = 16
MXU_TILE = 256
VMEM_LIMIT = 56 * 1024 * 1024
VMEM_LIMIT_TAIL = 60 * 1024 * 1024

PROJ_TILE = 256
FF_CHUNK = 512
SCAN_STEPS = 64
PERM_PITCH = SCAN_STEPS + SUBLANES
ATT_BAND = 4
ATT_KEY_ROWS = 12
HEADS_PER_VREG = LANES // HEAD_DIM


def _rms(x, g):
    ms = jnp.mean(x * x, axis=-1, keepdims=True)
    return x * lax.rsqrt(ms + EPS) * g


def _sigmoid(x):
    return 1.0 / (1.0 + jnp.exp2(x * (-LOG2E)))


def _gelu_tanh(x):
    c = float(np.sqrt(2.0 / np.pi))
    return 0.5 * x * (1.0 + jnp.tanh(c * (x + 0.044715 * (x * x * x))))


def _softplus(x):
    return jnp.maximum(x, 0.0) + jnp.log1p(jnp.exp(-jnp.abs(x)))


def _const_spec(shape):
    nd = len(shape)
    return pl.BlockSpec(shape, lambda *_: (0,) * nd, pipeline_mode=pl.Buffered(1))


def _stack_batches(ref, c, lo=0, width=None):
    width = c if width is None else width
    return jnp.concatenate([ref[:, b * c + lo:b * c + lo + width] for b in range(SUBLANES)], axis=0)


def _store_batches(ref, val, c, lo=0):
    steps = ref.shape[0]
    width = val.shape[1]
    for b in range(SUBLANES):
        ref[:, b * c + lo:b * c + lo + width] = val[b * steps:(b + 1) * steps].astype(ref.dtype)


def _halo_perm_matrices():
    prev_steps = CONV_W // 2
    halo_rows = SUBLANES * SUBLANES
    pp = np.zeros((prev_steps * SUBLANES, halo_rows), np.float32)
    pn = np.zeros((SUBLANES, halo_rows), np.float32)
    for b in range(SUBLANES):
        for j in range(prev_steps):
            pp[j * SUBLANES + b, b * SUBLANES + SUBLANES - prev_steps + j] = 1.0
        pn[b, b * SUBLANES] = 1.0
    return pp, pn


def _inproj_fwd_kernel(xprev_ref, x_ref, xnext_ref, g_ref, w_ref, b_ref, pp_ref, pn_ref,
                       cw_ref, cb_ref, wa_ref, ba_ref, wi_ref, bi_ref, lam_ref,
                       q_ref, k_ref, v_ref, sga_ref, sgr_ref, u_ref, gy_ref, hf_ref, carry, perm_s, wa_t, wi_t,
                       *, d_rec, d_model):
    t = pl.program_id(0)
    n_t = pl.num_programs(0)
    rows = SCAN_STEPS * SUBLANES
    prev_rows = (CONV_W // 2) * SUBLANES

    @pl.when(t == 0)
    def _():
        carry[...] = jnp.zeros_like(carry)
        _build_gate_tiles(wa_ref, wa_t)
        _build_gate_tiles(wi_ref, wi_t)

    def normed(ref):
        return _rms(ref[...].reshape(-1, d_model), g_ref[...])

    def permuted(p, hb):
        return jnp.dot(p[...], hb, preferred_element_type=F32).astype(BF16)

    def proj(lhs, lo, width):
        return jnp.dot(lhs, w_ref[:, lo:lo + width], preferred_element_type=F32) + b_ref[:, lo:lo + width]

    def step_major(hb):
        for c in range(d_model // LANES):
            for b in range(SUBLANES):
                perm_s[c, b * PERM_PITCH:b * PERM_PITCH + SCAN_STEPS, :] = (
                    hb[b * SCAN_STEPS:(b + 1) * SCAN_STEPS, c * LANES:(c + 1) * LANES])
        return jnp.concatenate(
            [jnp.concatenate([perm_s[c, pl.ds(ts, SUBLANES, stride=PERM_PITCH), :] for c in range(d_model // LANES)],
                             axis=1) for ts in range(SCAN_STEPS)], axis=0)

    h32 = normed(x_ref)
    h = h32.astype(BF16)
    hp = step_major(h32).astype(BF16)
    lhs_u = jnp.concatenate([permuted(pp_ref, normed(xprev_ref).astype(BF16)), hp,
                             permuted(pn_ref, normed(xnext_ref).astype(BF16))], axis=0)
    lo_u = 3 * D_ATT
    lo_y = lo_u + d_rec
    lo_ga = lo_y + d_rec
    lo_gr = lo_ga + d_model

    def u_tile(kt):
        return proj(lhs_u, lo_u + kt * MXU_TILE, MXU_TILE)

    def rec_tile(kt, ucat):
        cs = slice(kt * MXU_TILE, (kt + 1) * MXU_TILE)
        u_ref[:, cs] = ucat[prev_rows:prev_rows + rows].astype(BF16)
        ucat = jnp.concatenate([jnp.where(t == 0, 0.0, ucat[:prev_rows]), ucat[prev_rows:prev_rows + rows],
                                jnp.where(t == n_t - 1, 0.0, ucat[prev_rows + rows:])], axis=0)
        h_fwd = _rglru_tile(ucat, kt, cw_ref, cb_ref, wa_t, ba_ref, wi_t, bi_ref, lam_ref, carry, reverse=False)
        hf_ref[:, cs] = h_fwd.astype(BF16)

    def att_tile(ref, lo, kt, scale):
        val = proj(h, lo + kt * PROJ_TILE, PROJ_TILE)
        _store_batches(ref, val if scale is None else val * scale, D_ATT, kt * PROJ_TILE)

    def gy_tile(kt):
        cs = slice(kt * PROJ_TILE, (kt + 1) * PROJ_TILE)
        gy_ref[:, cs] = _gelu_tanh(proj(hp, lo_y + kt * PROJ_TILE, PROJ_TILE)).astype(BF16)

    def gate_tile(ref, lo, kt):
        _store_batches(ref, _sigmoid(proj(h, lo + kt * PROJ_TILE, PROJ_TILE)), d_model, kt * PROJ_TILE)

    fillers = [functools.partial(att_tile, q_ref, 0, kt, HEAD_DIM ** -0.5 * LOG2E) for kt in range(D_ATT // PROJ_TILE)]
    fillers += [functools.partial(att_tile, k_ref, D_ATT, kt, None) for kt in range(D_ATT // PROJ_TILE)]
    fillers += [functools.partial(att_tile, v_ref, 2 * D_ATT, kt, None) for kt in range(D_ATT // PROJ_TILE)]
    fillers += [functools.partial(gy_tile, kt) for kt in range(d_rec // PROJ_TILE)]
    fillers += [functools.partial(gate_tile, sga_ref, lo_ga, kt) for kt in range(d_model // PROJ_TILE)]
    fillers += [functools.partial(gate_tile, sgr_ref, lo_gr, kt) for kt in range(d_model // PROJ_TILE)]
    n_rec = d_rec // MXU_TILE
    first = len(fillers) // n_rec - 1
    per_rec = -(-(len(fillers) - first) // (n_rec - 1))
    bounds = [0] + [min(first + i * per_rec, len(fillers)) for i in range(n_rec)]
    ucat_next = u_tile(0)
    for kt in range(n_rec):
        ucat = ucat_next
        if kt + 1 < n_rec:
            ucat_next = u_tile(kt + 1)
        for f in fillers[bounds[kt]:bounds[kt + 1]]:
            f()
        rec_tile(kt, ucat)


def _inproj_fwd(x, g, w, b, perms, rec_params, d_rec):
    bsz, s, d = x.shape
    rows = SCAN_STEPS * SUBLANES
    halo_blocks = SCAN_STEPS // SUBLANES
    n_halo = s // SUBLANES

    def tm(width):
        return jax.ShapeDtypeStruct((s, bsz * width), BF16), pl.BlockSpec((SCAN_STEPS, bsz * width), lambda t: (t, 0))

    def sm(width):
        return jax.ShapeDtypeStruct((s * bsz, width), BF16), pl.BlockSpec((rows, width), lambda t: (t, 0))

    outs = [tm(D_ATT), tm(D_ATT), tm(D_ATT), tm(d), tm(d), sm(d_rec), sm(d_rec), sm(d_rec)]
    consts = [g, w, b, *perms, *rec_params]
    return pl.pallas_call(
        functools.partial(_inproj_fwd_kernel, d_rec=d_rec, d_model=d),
        grid=(s // SCAN_STEPS,),
        in_specs=[
            pl.BlockSpec((bsz, SUBLANES, d), lambda t: (0, jnp.maximum(t * halo_blocks - 1, 0), 0)),
            pl.BlockSpec((bsz, SCAN_STEPS, d), lambda t: (0, t, 0)),
            pl.BlockSpec((bsz, SUBLANES, d), lambda t: (0, jnp.minimum((t + 1) * halo_blocks, n_halo - 1), 0)),
        ] + [_const_spec(c.shape) for c in consts],
        out_specs=[o[1] for o in outs],
        out_shape=[o[0] for o in outs],
        scratch_shapes=[pltpu.VMEM((SUBLANES, d_rec), F32),
                        pltpu.VMEM((d // LANES, SUBLANES * PERM_PITCH, LANES), F32),
                        _gate_tile_scratch(rec_params[2]), _gate_tile_scratch(rec_params[4])],
        compiler_params=pltpu.CompilerParams(dimension_semantics=("arbitrary",), vmem_limit_bytes=VMEM_LIMIT),
        name="inproj_rglru_fwd",
    )(x, x, x, *consts)


def _attn_kernel(q_ref, k0_ref, k1_ref, k2_ref, v0_ref, v1_ref, v2_ref, tlo_ref, thi_ref, o_ref, bias_ref,
                 *, n_rows):
    band = pl.program_id(0)
    rows_q = ATT_BAND * GRID_W
    lane_row = lax.broadcasted_iota(jnp.int32, (GRID_W, LANES), 1) < HEAD_DIM

    @pl.when(pl.program_id(1) == 0)
    def _():
        r0 = band * ATT_BAND
        k0 = jnp.clip(r0 - WIN_H // 2, 0, n_rows - ATT_KEY_ROWS)
        for i in range(ATT_BAND):
            rq = r0 + i
            rs = jnp.clip(rq - WIN_H // 2, 0, n_rows - WIN_H)
            for jp in range(ATT_KEY_ROWS // HEADS_PER_VREG):
                rka = k0 + 2 * jp
                rkb = rka + 1
                va = jnp.logical_and(rka >= rs, rka < rs + WIN_H)
                vb = jnp.logical_and(rkb >= rs, rkb < rs + WIN_H)
                dra = jnp.clip(rka - rq + (WIN_H - 1), 0, 2 * WIN_H - 2)
                drb = jnp.clip(rkb - rq + (WIN_H - 1), 0, 2 * WIN_H - 2)
                for h in range(N_HEADS):
                    a = jnp.where(va, tlo_ref[h, dra], NEG)
                    b = jnp.where(vb, thi_ref[h, drb], NEG)
                    bias_ref[h, i * GRID_W:(i + 1) * GRID_W, jp * LANES:(jp + 1) * LANES] = jnp.where(lane_row, a, b)

    first = lax.broadcasted_iota(jnp.int32, (rows_q, LANES), 1) < HEAD_DIM
    k_refs = (k0_ref, k1_ref, k2_ref)
    v_refs = (v0_ref, v1_ref, v2_ref)
    zero = jnp.zeros((), BF16)

    def scores(hd):
        j, half = divmod(hd, HEADS_PER_VREG)
        cs = slice(j * LANES, (j + 1) * LANES)
        qh = jnp.where(first if half == 0 else jnp.logical_not(first), q_ref[:, cs], zero)
        s = jnp.concatenate(
            [lax.dot_general(qh, r[:, cs], (((1,), (1,)), ((), ())), preferred_element_type=F32) for r in k_refs],
            axis=1) + bias_ref[hd]
        return s, jnp.max(s, axis=-1, keepdims=True)

    nxt = scores(0)
    outs = []
    for hd in range(N_HEADS):
        j = hd // HEADS_PER_VREG
        cs = slice(j * LANES, (j + 1) * LANES)
        s, m = nxt
        if hd + 1 < N_HEADS:
            nxt = scores(hd + 1)
        p = jnp.exp2(s - m)
        l = jnp.sum(p, axis=-1, keepdims=True)
        pb = p.astype(BF16)
        o = jnp.dot(pb[:, 0:MXU_TILE], v_refs[0][:, cs], preferred_element_type=F32)
        for jj in range(1, len(v_refs)):
            o = o + jnp.dot(pb[:, jj * MXU_TILE:(jj + 1) * MXU_TILE], v_refs[jj][:, cs], preferred_element_type=F32)
        outs.append(o / l)
        if len(outs) == HEADS_PER_VREG:
            o_ref[:, cs] = jnp.where(first, outs[0], outs[1]).astype(BF16)
            outs = []


def _attention(q, k, v, tlo, thi, bsz):
    s = q.shape[0]
    n_rows = s // GRID_W
    n_bands = n_rows // ATT_BAND
    rows_q = ATT_BAND * GRID_W
    n_kblk = ATT_KEY_ROWS * GRID_W // MXU_TILE
    assert rows_q == MXU_TILE and ATT_KEY_ROWS * GRID_W == n_kblk * MXU_TILE

    def kv_spec(jj):
        def idx(band, bi):
            return (jnp.clip(band - 1, 0, n_bands - n_kblk) + jj, bi)
        return pl.BlockSpec((MXU_TILE, D_ATT), idx)

    return pl.pallas_call(
        functools.partial(_attn_kernel, n_rows=n_rows),
        grid=(n_bands, bsz),
        in_specs=[pl.BlockSpec((rows_q, D_ATT), lambda band, bi: (band, bi))]
        + [kv_spec(jj) for jj in range(n_kblk)] * 2
        + [_const_spec(tlo.shape), _const_spec(thi.shape)],
        out_specs=pl.BlockSpec((rows_q, D_ATT), lambda band, bi: (band, bi)),
        out_shape=jax.ShapeDtypeStruct((s, bsz * D_ATT), BF16),
        scratch_shapes=[pltpu.VMEM((N_HEADS, rows_q, ATT_KEY_ROWS * GRID_W), F32)],
        compiler_params=pltpu.CompilerParams(
            dimension_semantics=("arbitrary", "arbitrary"), vmem_limit_bytes=VMEM_LIMIT),
        name="nbr_attention",
    )(q, k, k, k, v, v, v, tlo, thi)


def _bias_tables(rpb):
    qc = np.arange(GRID_W)[:, None]
    kc = np.arange(GRID_W)[None, :]
    ws = np.clip(qc - WIN_W // 2, 0, GRID_W - WIN_W)
    valid = (kc >= ws) & (kc < ws + WIN_W)
    d_col = np.clip(kc - qc, -(WIN_W - 1), WIN_W - 1) + (WIN_W - 1)
    onehot = (np.arange(2 * WIN_W - 1)[:, None, None] == d_col[None]).astype(np.float32)
    t = jnp.einsum("hrd,dqk->hrqk", rpb.astype(F32), onehot, precision=lax.Precision.HIGHEST)
    t = jnp.where(valid[None, None], t * LOG2E, NEG)
    neg = jnp.full_like(t, NEG)
    return jnp.concatenate([t, neg], axis=-1), jnp.concatenate([neg, t], axis=-1)


def _rglru_tile(ucat, kt, cw_ref, cb_ref, wa_ref, ba_ref, wi_ref, bi_ref, lam_ref, carry, reverse):
    rows = SCAN_STEPS * SUBLANES
    cs = slice(kt * MXU_TILE, (kt + 1) * MXU_TILE)
    conv = cb_ref[:, cs]
    for j in range(CONV_W):
        conv = conv + ucat[j * SUBLANES:j * SUBLANES + rows] * cw_ref[j:j + 1, cs]
    conv_b = conv.astype(BF16)

    def gate(w_ref, b_ref):
        return _sigmoid(jnp.dot(conv_b, w_ref[kt], preferred_element_type=F32) + b_ref[:, cs])

    r_gate = gate(wa_ref, ba_ref)
    i_gate = gate(wi_ref, bi_ref)
    log_a = r_gate * (-LRU_C * _softplus(-lam_ref[:, cs]))
    a = jnp.exp(log_a)
    z = jnp.maximum(jnp.tanh(log_a) * (-1.0 - a * a), 0.0)
    mult = jnp.where(z > 0.0, z * lax.rsqrt(z), 0.0)
    bx = mult * (i_gate * conv)

    h = carry[:, cs]
    hs = [None] * SCAN_STEPS
    for i in range(SCAN_STEPS):
        t = SCAN_STEPS - 1 - i if reverse else i
        sl = slice(t * SUBLANES, (t + 1) * SUBLANES)
        h = a[sl] * h + bx[sl]
        hs[t] = h
    carry[:, cs] = h
    return jnp.concatenate(hs, axis=0)


def _tail_kernel(uprev_ref, u_ref, unext_ref, cw_ref, cb_ref, wa_ref, ba_ref, wi_ref, bi_ref, lam_ref,
                 gy_ref, hf_ref, x_ref, att_ref, sga_ref, sgr_ref, wao_ref, wro_ref, wout_ref,
                 g2_ref, w1_ref, w2_ref, gf_ref, o_ref, carry, y_s, wa_t, wi_t, *, ff_chunk):
    j = pl.program_id(0)
    n_t = pl.num_programs(0) - 1
    rows = SCAN_STEPS * SUBLANES
    d = x_ref.shape[-1]
    prev_rows = (CONV_W // 2) * SUBLANES
    t_idx = n_t - 1 - jnp.minimum(j, n_t - 1)
    n_tiles = d // MXU_TILE

    def rec_tile(kt):
        cs = slice(kt * MXU_TILE, (kt + 1) * MXU_TILE)
        prev = jnp.where(t_idx == 0, 0.0, uprev_ref[:, cs].astype(F32)[BF16_ROWS - prev_rows:])
        after = jnp.where(t_idx == n_t - 1, 0.0, unext_ref[:, cs].astype(F32)[0:SUBLANES])
        ucat = jnp.concatenate([prev, u_ref[:, cs].astype(F32), after], axis=0)
        h_bwd = _rglru_tile(ucat, kt, cw_ref, cb_ref, wa_t, ba_ref, wi_t, bi_ref, lam_ref, carry, reverse=True)
        y = (hf_ref[:, cs].astype(F32) + h_bwd) * gy_ref[:, cs].astype(F32)
        for ts in range(SCAN_STEPS):
            for ci in range(MXU_TILE // LANES):
                y_s[kt * (MXU_TILE // LANES) + ci, pl.ds(ts, SUBLANES, stride=PERM_PITCH), :] = (
                    y[ts * SUBLANES:(ts + 1) * SUBLANES, ci * LANES:(ci + 1) * LANES])

    @pl.when(j == 0)
    def _():
        carry[...] = jnp.zeros_like(carry)
        _build_gate_tiles(wa_ref, wa_t)
        _build_gate_tiles(wi_ref, wi_t)
        for kt in range(n_tiles):
            rec_tile(kt)

    @pl.when(j > 0)
    def _():
        y_nat = jnp.concatenate(
            [jnp.concatenate([y_s[c, b * PERM_PITCH:b * PERM_PITCH + SCAN_STEPS, :] for c in range(d // LANES)],
                             axis=1) for b in range(SUBLANES)], axis=0).astype(BF16)
        y_rec = jnp.dot(y_nat, wro_ref[...], preferred_element_type=F32)
        y_att = jnp.dot(_stack_batches(att_ref, D_ATT), wao_ref[...], preferred_element_type=F32)
        mixed = _stack_batches(sga_ref, d).astype(F32) * y_att + _stack_batches(sgr_ref, d).astype(F32) * y_rec
        out = x_ref[...].reshape(rows, d) + jnp.dot(mixed.astype(BF16), wout_ref[...], preferred_element_type=F32)

        h2 = _rms(out, g2_ref[...]).astype(BF16)
        d_ff = w1_ref.shape[1]
        n_chunks = d_ff // ff_chunk
        rec_every = max(n_chunks // n_tiles, 1)
        for c in range(max(n_chunks, n_tiles * rec_every)):
            if c < n_chunks:
                fs = slice(c * ff_chunk, (c + 1) * ff_chunk)
                hid = jnp.maximum(jnp.dot(h2, w1_ref[:, fs], preferred_element_type=F32), 0.0)
                out = out + jnp.dot((hid * hid).astype(BF16), w2_ref[fs, :], preferred_element_type=F32)
            if c % rec_every == 0 and c // rec_every < n_tiles:
                rec_tile(c // rec_every)
        o_ref[...] = _rms(out, gf_ref[...]).reshape(o_ref.shape)


def _tail(u, params, gy, h_fwd, x, att, sga, sgr, wao, wro, wout, g2, w1, w2, gf):
    n_rows, c = u.shape
    bsz, s, d = x.shape
    rows = SCAN_STEPS * SUBLANES
    n_t = s // SCAN_STEPS
    halo_per_block = rows // BF16_ROWS
    n_halo = n_rows // BF16_ROWS

    def scan_blk(j):
        return n_t - 1 - jnp.minimum(j, n_t - 1)

    def merge_blk(j):
        return n_t - 1 - jnp.maximum(j - 1, 0)

    sm_spec = pl.BlockSpec((rows, c), lambda j: (scan_blk(j), 0))
    halo_specs = [
        pl.BlockSpec((BF16_ROWS, c), lambda j: (jnp.maximum(scan_blk(j) * halo_per_block - 1, 0), 0)),
        sm_spec,
        pl.BlockSpec((BF16_ROWS, c), lambda j: (jnp.minimum((scan_blk(j) + 1) * halo_per_block, n_halo - 1), 0)),
    ]

    def tm(width):
        return pl.BlockSpec((SCAN_STEPS, bsz * width), lambda j: (merge_blk(j), 0))

    x_spec = pl.BlockSpec((bsz, SCAN_STEPS, d), lambda j: (0, merge_blk(j), 0))
    weights = [wao, wro, wout, g2, w1, w2, gf]
    return pl.pallas_call(
        functools.partial(_tail_kernel, ff_chunk=FF_CHUNK),
        grid=(n_t + 1,),
        in_specs=halo_specs + [_const_spec(p.shape) for p in params] + [sm_spec, sm_spec]
        + [x_spec, tm(D_ATT), tm(d), tm(d)]
        + [_const_spec(p.shape) for p in weights],
        out_specs=x_spec,
        out_shape=jax.ShapeDtypeStruct(x.shape, F32),
        scratch_shapes=[pltpu.VMEM((SUBLANES, c), F32), pltpu.VMEM((c // LANES, SUBLANES * PERM_PITCH, LANES), F32),
                        _gate_tile_scratch(params[2]), _gate_tile_scratch(params[4])],
        compiler_params=pltpu.CompilerParams(dimension_semantics=("arbitrary",), vmem_limit_bytes=VMEM_LIMIT_TAIL),
        name="rglru_bwd_merge_ffn",
    )(u, u, u, *params, gy, h_fwd, x, att, sga, sgr, *weights)


def _build_gate_tiles(w_ref, tile_ref):
    n_blocks, bw, _ = w_ref.shape
    per = MXU_TILE // bw
    tile_ref[...] = jnp.zeros_like(tile_ref)
    for n in range(n_blocks):
        kt, p = divmod(n, per)
        tile_ref[kt, p * bw:(p + 1) * bw, p * bw:(p + 1) * bw] = w_ref[n].astype(tile_ref.dtype)


def _gate_tile_scratch(w):
    n_blocks, bw, _ = w.shape
    return pltpu.VMEM((n_blocks * bw // MXU_TILE, MXU_TILE, MXU_TILE), BF16)


def kernel(x, ln1_g, w_in, b_in, rpb, w_att_o, conv_w, conv_b, w_rg_a, b_rg_a, w_rg_i, b_rg_i, lru_lambda,
           w_rec_o, w_out, ln2_g, w_ff1, w_ff2, lnf_g):
    bsz, s, d = x.shape
    d_rec = conv_w.shape[-1]
    assert ln1_g.shape[0] == 1, "single-layer stack only"
    assert bsz == SUBLANES and s % SCAN_STEPS == 0 and s % (ATT_BAND * GRID_W) == 0 and w_ff1.shape[-1] % FF_CHUNK == 0
    assert w_in.shape[-1] == 3 * D_ATT + 2 * d_rec + 2 * d and d_rec % MXU_TILE == 0 and d_rec == d
    row = lambda p: p.reshape(1, -1).astype(F32)
    l = 0

    def dir_params(di):
        return [conv_w[l].astype(F32), row(conv_b[l]), w_rg_a[l, di].astype(F32), row(b_rg_a[l, di]),
                w_rg_i[l, di].astype(F32), row(b_rg_i[l, di]), row(lru_lambda[l, di])]

    perms = [jnp.asarray(m, BF16) for m in _halo_perm_matrices()]
    q, k, v, sga, sgr, u, gy, h_fwd = _inproj_fwd(x, row(ln1_g[l]), w_in[l].astype(BF16), row(b_in[l]), perms,
                                                  dir_params(0), d_rec)

    tlo, thi = _bias_tables(rpb[l])
    att = _attention(q, k, v, tlo, thi, bsz)

    return _tail(u, dir_params(1), gy, h_fwd, x, att, sga, sgr,
                 w_att_o[l].astype(BF16), w_rec_o[l].astype(BF16), w_out[l].astype(BF16),
                 row(ln2_g[l]), w_ff1[l].astype(BF16), w_ff2[l].astype(BF16), row(lnf_g))
```

```python
import functools

import jax
import jax.numpy as jnp
import numpy as np
from jax import lax
from jax.experimental import pallas as pl
from jax.experimental.pallas import tpu as pltpu

F32 = jnp.float32
BF16 = jnp.bfloat16

GRID_W = 64
N_HEADS = 8
HEAD_DIM = 64
D_ATT = N_HEADS * HEAD_DIM
WIN_H = 8
WIN_W = 16
CONV_W = 4
LRU_C = 8.0
EPS = 1e-6
NEG = -1e30
LOG2E = float(np.log2(np.e))

LANES = 128
SUBLANES = 8
BF16_ROWS = 16
MXU_TILE = 256
VMEM_LIMIT = 56 * 1024 * 1024
VMEM_LIMIT_TAIL = 60 * 1024 * 1024

PROJ_TILE = 256
FF_CHUNK = 512
SCAN_STEPS = 64
PERM_PITCH = SCAN_STEPS + SUBLANES
ATT_BAND = 4
ATT_KEY_ROWS = 12
HEADS_PER_VREG = LANES // HEAD_DIM


def _rms(x, g):
    ms = jnp.mean(x * x, axis=-1, keepdims=True)
    return x * lax.rsqrt(ms + EPS) * g


def _sigmoid(x):
    return 1.0 / (1.0 + jnp.exp2(x * (-LOG2E)))


def _gelu_tanh(x):
    c = float(np.sqrt(2.0 / np.pi))
    return 0.5 * x * (1.0 + jnp.tanh(c * (x + 0.044715 * (x * x * x))))


def _softplus(x):
    return jnp.maximum(x, 0.0) + jnp.log1p(jnp.exp(-jnp.abs(x)))


def _const_spec(shape):
    nd = len(shape)
    return pl.BlockSpec(shape, lambda *_: (0,) * nd, pipeline_mode=pl.Buffered(1))


def _stack_batches(ref, c, lo=0, width=None):
    width = c if width is None else width
    return jnp.concatenate([ref[:, b * c + lo:b * c + lo + width] for b in range(SUBLANES)], axis=0)


def _store_batches(ref, val, c, lo=0):
    steps = ref.shape[0]
    width = val.shape[1]
    for b in range(SUBLANES):
        ref[:, b * c + lo:b * c + lo + width] = val[b * steps:(b + 1) * steps].astype(ref.dtype)


def _halo_perm_matrices():
    prev_steps = CONV_W // 2
    halo_rows = SUBLANES * SUBLANES
    pp = np.zeros((prev_steps * SUBLANES, halo_rows), np.float32)
    pn = np.zeros((SUBLANES, halo_rows), np.float32)
    for b in range(SUBLANES):
        for j in range(prev_steps):
            pp[j * SUBLANES + b, b * SUBLANES + SUBLANES - prev_steps + j] = 1.0
        pn[b, b * SUBLANES] = 1.0
    return pp, pn


def _inproj_fwd_kernel(xprev_ref, x_ref, xnext_ref, g_ref, w_ref, b_ref, pp_ref, pn_ref,
                       cw_ref, cb_ref, wa_ref, ba_ref, wi_ref, bi_ref, lam_ref,
                       q_ref, k_ref, v_ref, sga_ref, sgr_ref, u_ref, gy_ref, hf_ref, carry, perm_s, wa_t, wi_t,
                       *, d_rec, d_model):
    t = pl.program_id(0)
    n_t = pl.num_programs(0)
    rows = SCAN_STEPS * SUBLANES
    prev_rows = (CONV_W // 2) * SUBLANES

    @pl.when(t == 0)
    def _():
        carry[...] = jnp.zeros_like(carry)
        _build_gate_tiles(wa_ref, wa_t)
        _build_gate_tiles(wi_ref, wi_t)

    def normed(ref):
        return _rms(ref[...].reshape(-1, d_model), g_ref[...])

    def permuted(p, hb):
        return jnp.dot(p[...], hb, preferred_element_type=F32).astype(BF16)

    def proj(lhs, lo, width):
        return jnp.dot(lhs, w_ref[:, lo:lo + width], preferred_element_type=F32) + b_ref[:, lo:lo + width]

    def step_major(hb):
        for c in range(d_model // LANES):
            for b in range(SUBLANES):
                perm_s[c, b * PERM_PITCH:b * PERM_PITCH + SCAN_STEPS, :] = (
                    hb[b * SCAN_STEPS:(b + 1) * SCAN_STEPS, c * LANES:(c + 1) * LANES])
        return jnp.concatenate(
            [jnp.concatenate([perm_s[c, pl.ds(ts, SUBLANES, stride=PERM_PITCH), :] for c in range(d_model // LANES)],
                             axis=1) for ts in range(SCAN_STEPS)], axis=0)

    h32 = normed(x_ref)
    h = h32.astype(BF16)
    hp = step_major(h32).astype(BF16)
    lhs_u = jnp.concatenate([permuted(pp_ref, normed(xprev_ref).astype(BF16)), hp,
                             permuted(pn_ref, normed(xnext_ref).astype(BF16))], axis=0)
    lo_u = 3 * D_ATT
    lo_y = lo_u + d_rec
    lo_ga = lo_y + d_rec
    lo_gr = lo_ga + d_model

    def u_tile(kt):
        return proj(lhs_u, lo_u + kt * MXU_TILE, MXU_TILE)

    def rec_tile(kt, ucat):
        cs = slice(kt * MXU_TILE, (kt + 1) * MXU_TILE)
        u_ref[:, cs] = ucat[prev_rows:prev_rows + rows].astype(BF16)
        ucat = jnp.concatenate([jnp.where(t == 0, 0.0, ucat[:prev_rows]), ucat[prev_rows:prev_rows + rows],
                                jnp.where(t == n_t - 1, 0.0, ucat[prev_rows + rows:])], axis=0)
        h_fwd = _rglru_tile(ucat, kt, cw_ref, cb_ref, wa_t, ba_ref, wi_t, bi_ref, lam_ref, carry, reverse=False)
        hf_ref[:, cs] = h_fwd.astype(BF16)

    def att_tile(ref, lo, kt, scale):
        val = proj(h, lo + kt * PROJ_TILE, PROJ_TILE)
        _store_batches(ref, val if scale is None else val * scale, D_ATT, kt * PROJ_TILE)

    def gy_tile(kt):
        cs = slice(kt * PROJ_TILE, (kt + 1) * PROJ_TILE)
        gy_ref[:, cs] = _gelu_tanh(proj(hp, lo_y + kt * PROJ_TILE, PROJ_TILE)).astype(BF16)

    def gate_tile(ref, lo, kt):
        _store_batches(ref, _sigmoid(proj(h, lo + kt * PROJ_TILE, PROJ_TILE)), d_model, kt * PROJ_TILE)

    fillers = [functools.partial(att_tile, q_ref, 0, kt, HEAD_DIM ** -0.5 * LOG2E) for kt in range(D_ATT // PROJ_TILE)]
    fillers += [functools.partial(att_tile, k_ref, D_ATT, kt, None) for kt in range(D_ATT // PROJ_TILE)]
    fillers += [functools.partial(att_tile, v_ref, 2 * D_ATT, kt, None) for kt in range(D_ATT // PROJ_TILE)]
    fillers += [functools.partial(gy_tile, kt) for kt in range(d_rec // PROJ_TILE)]
    fillers += [functools.partial(gate_tile, sga_ref, lo_ga, kt) for kt in range(d_model // PROJ_TILE)]
    fillers += [functools.partial(gate_tile, sgr_ref, lo_gr, kt) for kt in range(d_model // PROJ_TILE)]
    n_rec = d_rec // MXU_TILE
    first = len(fillers) // n_rec - 1
    per_rec = -(-(len(fillers) - first) // (n_rec - 1))
    bounds = [0] + [min(first + i * per_rec, len(fillers)) for i in range(n_rec)]
    ucat_next = u_tile(0)
    for kt in range(n_rec):
        ucat = ucat_next
        if kt + 1 < n_rec:
            ucat_next = u_tile(kt + 1)
        for f in fillers[bounds[kt]:bounds[kt + 1]]:
            f()
        rec_tile(kt, ucat)


def _inproj_fwd(x, g, w, b, perms, rec_params, d_rec):
    bsz, s, d = x.shape
    rows = SCAN_STEPS * SUBLANES
    halo_blocks = SCAN_STEPS // SUBLANES
    n_halo = s // SUBLANES

    def tm(width):
        return jax.ShapeDtypeStruct((s, bsz * width), BF16), pl.BlockSpec((SCAN_STEPS, bsz * width), lambda t: (t, 0))

    def sm(width):
        return jax.ShapeDtypeStruct((s * bsz, width), BF16), pl.BlockSpec((rows, width), lambda t: (t, 0))

    outs = [tm(D_ATT), tm(D_ATT), tm(D_ATT), tm(d), tm(d), sm(d_rec), sm(d_rec), sm(d_rec)]
    consts = [g, w, b, *perms, *rec_params]
    return pl.pallas_call(
        functools.partial(_inproj_fwd_kernel, d_rec=d_rec, d_model=d),
        grid=(s // SCAN_STEPS,),
        in_specs=[
            pl.BlockSpec((bsz, SUBLANES, d), lambda t: (0, jnp.maximum(t * halo_blocks - 1, 0), 0)),
            pl.BlockSpec((bsz, SCAN_STEPS, d), lambda t: (0, t, 0)),
            pl.BlockSpec((bsz, SUBLANES, d), lambda t: (0, jnp.minimum((t + 1) * halo_blocks, n_halo - 1), 0)),
        ] + [_const_spec(c.shape) for c in consts],
        out_specs=[o[1] for o in outs],
        out_shape=[o[0] for o in outs],
        scratch_shapes=[pltpu.VMEM((SUBLANES, d_rec), F32),
                        pltpu.VMEM((d // LANES, SUBLANES * PERM_PITCH, LANES), F32),
                        _gate_tile_scratch(rec_params[2]), _gate_tile_scratch(rec_params[4])],
        compiler_params=pltpu.CompilerParams(dimension_semantics=("arbitrary",), vmem_limit_bytes=VMEM_LIMIT),
        name="inproj_rglru_fwd",
    )(x, x, x, *consts)


def _attn_kernel(q_ref, k0_ref, k1_ref, k2_ref, v0_ref, v1_ref, v2_ref, tlo_ref, thi_ref, *rest, n_rows, n_cast):
    w32_refs, o_ref, w16_refs, bias_ref = rest[:n_cast], rest[n_cast], rest[n_cast + 1:-1], rest[-1]
    for src, dst in zip(w32_refs, w16_refs):
        dst[...] = src[...].astype(dst.dtype)
    band = pl.program_id(0)
    rows_q = ATT_BAND * GRID_W
    lane_row = lax.broadcasted_iota(jnp.int32, (GRID_W, LANES), 1) < HEAD_DIM

    @pl.when(pl.program_id(1) == 0)
    def _():
        r0 = band * ATT_BAND
        k0 = jnp.clip(r0 - WIN_H // 2, 0, n_rows - ATT_KEY_ROWS)
        for i in range(ATT_BAND):
            rq = r0 + i
            rs = jnp.clip(rq - WIN_H // 2, 0, n_rows - WIN_H)
            for jp in range(ATT_KEY_ROWS // HEADS_PER_VREG):
                rka = k0 + 2 * jp
                rkb = rka + 1
                va = jnp.logical_and(rka >= rs, rka < rs + WIN_H)
                vb = jnp.logical_and(rkb >= rs, rkb < rs + WIN_H)
                dra = jnp.clip(rka - rq + (WIN_H - 1), 0, 2 * WIN_H - 2)
                drb = jnp.clip(rkb - rq + (WIN_H - 1), 0, 2 * WIN_H - 2)
                for h in range(N_HEADS):
                    a = jnp.where(va, tlo_ref[h, dra], NEG)
                    b = jnp.where(vb, thi_ref[h, drb], NEG)
                    bias_ref[h, i * GRID_W:(i + 1) * GRID_W, jp * LANES:(jp + 1) * LANES] = jnp.where(lane_row, a, b)

    first = lax.broadcasted_iota(jnp.int32, (rows_q, LANES), 1) < HEAD_DIM
    k_refs = (k0_ref, k1_ref, k2_ref)
    v_refs = (v0_ref, v1_ref, v2_ref)
    zero = jnp.zeros((), BF16)

    def scores(hd):
        j, half = divmod(hd, HEADS_PER_VREG)
        cs = slice(j * LANES, (j + 1) * LANES)
        qh = jnp.where(first if half == 0 else jnp.logical_not(first), q_ref[:, cs], zero)
        s = jnp.concatenate(
            [lax.dot_general(qh, r[:, cs], (((1,), (1,)), ((), ())), preferred_element_type=F32) for r in k_refs],
            axis=1) + bias_ref[hd]
        return s, jnp.max(s, axis=-1, keepdims=True)

    nxt = scores(0)
    outs = []
    for hd in range(N_HEADS):
        j = hd // HEADS_PER_VREG
        cs = slice(j * LANES, (j + 1) * LANES)
        s, m = nxt
        if hd + 1 < N_HEADS:
            nxt = scores(hd + 1)
        p = jnp.exp2(s - m)
        l = jnp.sum(p, axis=-1, keepdims=True)
        pb = p.astype(BF16)
        o = jnp.dot(pb[:, 0:MXU_TILE], v_refs[0][:, cs], preferred_element_type=F32)
        for jj in range(1, len(v_refs)):
            o = o + jnp.dot(pb[:, jj * MXU_TILE:(jj + 1) * MXU_TILE], v_refs[jj][:, cs], preferred_element_type=F32)
        outs.append(o / l)
        if len(outs) == HEADS_PER_VREG:
            o_ref[:, cs] = jnp.where(first, outs[0], outs[1]).astype(BF16)
            outs = []


def _attention(q, k, v, tlo, thi, bsz, cast_weights):
    s = q.shape[0]
    n_rows = s // GRID_W
    n_bands = n_rows // ATT_BAND
    rows_q = ATT_BAND * GRID_W
    n_kblk = ATT_KEY_ROWS * GRID_W // MXU_TILE
    assert rows_q == MXU_TILE and ATT_KEY_ROWS * GRID_W == n_kblk * MXU_TILE
    n_steps = n_bands * bsz

    def kv_spec(jj):
        def idx(band, bi):
            return (jnp.clip(band - 1, 0, n_bands - n_kblk) + jj, bi)
        return pl.BlockSpec((MXU_TILE, D_ATT), idx)

    def cast_spec(w):
        n_slices = min(n_steps, w.shape[0] // BF16_ROWS)
        assert w.shape[0] % n_slices == 0 and n_steps % n_slices == 0
        per = n_steps // n_slices
        return pl.BlockSpec((w.shape[0] // n_slices, w.shape[1]), lambda band, bi: ((band * bsz + bi) // per, 0))

    cast_specs = [cast_spec(w) for w in cast_weights]
    att_spec = pl.BlockSpec((rows_q, D_ATT), lambda band, bi: (band, bi))
    outs = pl.pallas_call(
        functools.partial(_attn_kernel, n_rows=n_rows, n_cast=len(cast_weights)),
        grid=(n_bands, bsz),
        in_specs=[att_spec] + [kv_spec(jj) for jj in range(n_kblk)] * 2
        + [_const_spec(tlo.shape), _const_spec(thi.shape)] + cast_specs,
        out_specs=[att_spec] + cast_specs,
        out_shape=[jax.ShapeDtypeStruct((s, bsz * D_ATT), BF16)]
        + [jax.ShapeDtypeStruct(w.shape, BF16) for w in cast_weights],
        scratch_shapes=[pltpu.VMEM((N_HEADS, rows_q, ATT_KEY_ROWS * GRID_W), F32)],
        compiler_params=pltpu.CompilerParams(
            dimension_semantics=("arbitrary", "arbitrary"), vmem_limit_bytes=VMEM_LIMIT),
        name="nbr_attention",
    )(q, k, k, k, v, v, v, tlo, thi, *cast_weights)
    return outs[0], outs[1:]


def _bias_tables(rpb):
    qc = np.arange(GRID_W)[:, None]
    kc = np.arange(GRID_W)[None, :]
    ws = np.clip(qc - WIN_W // 2, 0, GRID_W - WIN_W)
    valid = (kc >= ws) & (kc < ws + WIN_W)
    d_col = np.clip(kc - qc, -(WIN_W - 1), WIN_W - 1) + (WIN_W - 1)
    onehot = (np.arange(2 * WIN_W - 1)[:, None, None] == d_col[None]).astype(np.float32)
    t = jnp.einsum("hrd,dqk->hrqk", rpb.astype(F32), onehot, precision=lax.Precision.HIGHEST)
    t = jnp.where(valid[None, None], t * LOG2E, NEG)
    neg = jnp.full_like(t, NEG)
    return jnp.concatenate([t, neg], axis=-1), jnp.concatenate([neg, t], axis=-1)


def _rglru_tile(ucat, kt, cw_ref, cb_ref, wa_ref, ba_ref, wi_ref, bi_ref, lam_ref, carry, reverse):
    rows = SCAN_STEPS * SUBLANES
    cs = slice(kt * MXU_TILE, (kt + 1) * MXU_TILE)
    conv = cb_ref[:, cs]
    for j in range(CONV_W):
        conv = conv + ucat[j * SUBLANES:j * SUBLANES + rows] * cw_ref[j:j + 1, cs]
    conv_b = conv.astype(BF16)

    def gate(w_ref, b_ref):
        return _sigmoid(jnp.dot(conv_b, w_ref[kt], preferred_element_type=F32) + b_ref[:, cs])

    r_gate = gate(wa_ref, ba_ref)
    i_gate = gate(wi_ref, bi_ref)
    log_a = r_gate * (-LRU_C * _softplus(-lam_ref[:, cs]))
    a = jnp.exp(log_a)
    z = jnp.maximum(jnp.tanh(log_a) * (-1.0 - a * a), 0.0)
    mult = jnp.where(z > 0.0, z * lax.rsqrt(z), 0.0)
    bx = mult * (i_gate * conv)

    h = carry[:, cs]
    hs = [None] * SCAN_STEPS
    for i in range(SCAN_STEPS):
        t = SCAN_STEPS - 1 - i if reverse else i
        sl = slice(t * SUBLANES, (t + 1) * SUBLANES)
        h = a[sl] * h + bx[sl]
        hs[t] = h
    carry[:, cs] = h
    return jnp.concatenate(hs, axis=0)


def _tail_kernel(uprev_ref, u_ref, unext_ref, cw_ref, cb_ref, wa_ref, ba_ref, wi_ref, bi_ref, lam_ref,
                 gy_ref, hf_ref, x_ref, att_ref, sga_ref, sgr_ref, wao_ref, wro_ref, wout_ref,
                 g2_ref, w1_ref, w2_ref, gf_ref, o_ref, carry, y_s, wa_t, wi_t, *, ff_chunk):
    j = pl.program_id(0)
    n_t = pl.num_programs(0) - 1
    rows = SCAN_STEPS * SUBLANES
    d = x_ref.shape[-1]
    prev_rows = (CONV_W // 2) * SUBLANES
    t_idx = n_t - 1 - jnp.minimum(j, n_t - 1)
    n_tiles = d // MXU_TILE

    def rec_tile(kt):
        cs = slice(kt * MXU_TILE, (kt + 1) * MXU_TILE)
        prev = jnp.where(t_idx == 0, 0.0, uprev_ref[:, cs].astype(F32)[BF16_ROWS - prev_rows:])
        after = jnp.where(t_idx == n_t - 1, 0.0, unext_ref[:, cs].astype(F32)[0:SUBLANES])
        ucat = jnp.concatenate([prev, u_ref[:, cs].astype(F32), after], axis=0)
        h_bwd = _rglru_tile(ucat, kt, cw_ref, cb_ref, wa_t, ba_ref, wi_t, bi_ref, lam_ref, carry, reverse=True)
        y = (hf_ref[:, cs].astype(F32) + h_bwd) * gy_ref[:, cs].astype(F32)
        for ts in range(SCAN_STEPS):
            for ci in range(MXU_TILE // LANES):
                y_s[kt * (MXU_TILE // LANES) + ci, pl.ds(ts, SUBLANES, stride=PERM_PITCH), :] = (
                    y[ts * SUBLANES:(ts + 1) * SUBLANES, ci * LANES:(ci + 1) * LANES])

    @pl.when(j == 0)
    def _():
        carry[...] = jnp.zeros_like(carry)
        _build_gate_tiles(wa_ref, wa_t)
        _build_gate_tiles(wi_ref, wi_t)
        for kt in range(n_tiles):
            rec_tile(kt)

    @pl.when(j > 0)
    def _():
        y_nat = jnp.concatenate(
            [jnp.concatenate([y_s[c, b * PERM_PITCH:b * PERM_PITCH + SCAN_STEPS, :] for c in range(d // LANES)],
                             axis=1) for b in range(SUBLANES)], axis=0).astype(BF16)
        y_rec = jnp.dot(y_nat, wro_ref[...], preferred_element_type=F32)
        y_att = jnp.dot(_stack_batches(att_ref, D_ATT), wao_ref[...], preferred_element_type=F32)
        mixed = _stack_batches(sga_ref, d).astype(F32) * y_att + _stack_batches(sgr_ref, d).astype(F32) * y_rec
        out = x_ref[...].reshape(rows, d) + jnp.dot(mixed.astype(BF16), wout_ref[...], preferred_element_type=F32)

        h2 = _rms(out, g2_ref[...]).astype(BF16)
        d_ff = w1_ref.shape[1]
        n_chunks = d_ff // ff_chunk
        rec_every = max(n_chunks // n_tiles, 1)
        for c in range(max(n_chunks, n_tiles * rec_every)):
            if c < n_chunks:
                fs = slice(c * ff_chunk, (c + 1) * ff_chunk)
                hid = jnp.maximum(jnp.dot(h2, w1_ref[:, fs], preferred_element_type=F32), 0.0)
                out = out + jnp.dot((hid * hid).astype(BF16), w2_ref[fs, :], preferred_element_type=F32)
            if c % rec_every == 0 and c // rec_every < n_tiles:
                rec_tile(c // rec_every)
        o_ref[...] = _rms(out, gf_ref[...]).reshape(o_ref.shape)


def _tail(u, params, gy, h_fwd, x, att, sga, sgr, wao, wro, wout, g2, w1, w2, gf):
    n_rows, c = u.shape
    bsz, s, d = x.shape
    rows = SCAN_STEPS * SUBLANES
    n_t = s // SCAN_STEPS
    halo_per_block = rows // BF16_ROWS
    n_halo = n_rows // BF16_ROWS

    def scan_blk(j):
        return n_t - 1 - jnp.minimum(j, n_t - 1)

    def merge_blk(j):
        return n_t - 1 - jnp.maximum(j - 1, 0)

    sm_spec = pl.BlockSpec((rows, c), lambda j: (scan_blk(j), 0))
    halo_specs = [
        pl.BlockSpec((BF16_ROWS, c), lambda j: (jnp.maximum(scan_blk(j) * halo_per_block - 1, 0), 0)),
        sm_spec,
        pl.BlockSpec((BF16_ROWS, c), lambda j: (jnp.minimum((scan_blk(j) + 1) * halo_per_block, n_halo - 1), 0)),
    ]

    def tm(width):
        return pl.BlockSpec((SCAN_STEPS, bsz * width), lambda j: (merge_blk(j), 0))

    x_spec = pl.BlockSpec((bsz, SCAN_STEPS, d), lambda j: (0, merge_blk(j), 0))
    weights = [wao, wro, wout, g2, w1, w2, gf]
    return pl.pallas_call(
        functools.partial(_tail_kernel, ff_chunk=FF_CHUNK),
        grid=(n_t + 1,),
        in_specs=halo_specs + [_const_spec(p.shape) for p in params] + [sm_spec, sm_spec]
        + [x_spec, tm(D_ATT), tm(d), tm(d)]
        + [_const_spec(p.shape) for p in weights],
        out_specs=x_spec,
        out_shape=jax.ShapeDtypeStruct(x.shape, F32),
        scratch_shapes=[pltpu.VMEM((SUBLANES, c), F32), pltpu.VMEM((c // LANES, SUBLANES * PERM_PITCH, LANES), F32),
                        _gate_tile_scratch(params[2]), _gate_tile_scratch(params[4])],
        compiler_params=pltpu.CompilerParams(dimension_semantics=("arbitrary",), vmem_limit_bytes=VMEM_LIMIT_TAIL),
        name="rglru_bwd_merge_ffn",
    )(u, u, u, *params, gy, h_fwd, x, att, sga, sgr, *weights)


def _build_gate_tiles(w_ref, tile_ref):
    n_blocks, bw, _ = w_ref.shape
    per = MXU_TILE // bw
    tile_ref[...] = jnp.zeros_like(tile_ref)
    for n in range(n_blocks):
        kt, p = divmod(n, per)
        tile_ref[kt, p * bw:(p + 1) * bw, p * bw:(p + 1) * bw] = w_ref[n].astype(tile_ref.dtype)


def _gate_tile_scratch(w):
    n_blocks, bw, _ = w.shape
    return pltpu.VMEM((n_blocks * bw // MXU_TILE, MXU_TILE, MXU_TILE), BF16)


def kernel(x, ln1_g, w_in, b_in, rpb, w_att_o, conv_w, conv_b, w_rg_a, b_rg_a, w_rg_i, b_rg_i, lru_lambda,
           w_rec_o, w_out, ln2_g, w_ff1, w_ff2, lnf_g):
    bsz, s, d = x.shape
    d_rec = conv_w.shape[-1]
    assert ln1_g.shape[0] == 1, "single-layer stack only"
    assert bsz == SUBLANES and s % SCAN_STEPS == 0 and s % (ATT_BAND * GRID_W) == 0 and w_ff1.shape[-1] % FF_CHUNK == 0
    assert w_in.shape[-1] == 3 * D_ATT + 2 * d_rec + 2 * d and d_rec % MXU_TILE == 0 and d_rec == d
    row = lambda p: p.reshape(1, -1).astype(F32)
    l = 0

    def dir_params(di):
        return [conv_w[l].astype(F32), row(conv_b[l]), w_rg_a[l, di].astype(F32), row(b_rg_a[l, di]),
                w_rg_i[l, di].astype(F32), row(b_rg_i[l, di]), row(lru_lambda[l, di])]

    perms = [jnp.asarray(m, BF16) for m in _halo_perm_matrices()]
    q, k, v, sga, sgr, u, gy, h_fwd = _inproj_fwd(x, row(ln1_g[l]), w_in[l].astype(BF16), row(b_in[l]), perms,
                                                  dir_params(0), d_rec)

    tlo, thi = _bias_tables(rpb[l])
    att, (wao, wro, wout, w1, w2) = _attention(
        q, k, v, tlo, thi, bsz, [w.astype(F32) for w in (w_att_o[l], w_rec_o[l], w_out[l], w_ff1[l], w_ff2[l])])

    return _tail(u, dir_params(1), gy, h_fwd, x, att, sga, sgr, wao, wro, wout, row(ln2_g[l]), w1, w2, row(lnf_g))
```
